```python
import math
import jax, jax.numpy as jnp
from jax import lax
import numpy as np

D_MODEL = 1024
BATCH = 2
SEQ = 8192
DEPTH = 2

N_MIXERS = 2
N_DSA_LAYERS = (DEPTH + 1) // 2
N_GLA_LAYERS = DEPTH // 2

DN_ALPHA = (2.0 * DEPTH) ** 0.25
DN_BETA = (8.0 * DEPTH) ** -0.25
LN_EPS = 1e-5
RMS_EPS = 1e-6

A_HEADS = 16
A_QK_DIM = 64
A_V_DIM = 64
A_LATENT = 256
IDX_HEADS = 8
IDX_DIM = 64
IDX_TOPK_MAX = 256
Q_BLOCK = 128
D_IN_A = A_HEADS * A_QK_DIM + A_LATENT + IDX_HEADS * IDX_DIM + IDX_DIM + IDX_HEADS

B_HEADS = 4
B_KEY_DIM = D_MODEL // 2
B_VAL_DIM = D_MODEL
B_KH = B_KEY_DIM // B_HEADS
B_VH = B_VAL_DIM // B_HEADS
B_GATE_RANK = 16
B_GATE_TAU = 16.0
B_CHUNK = 64
D_IN_B = 2 * B_KEY_DIM + B_VAL_DIM + B_GATE_RANK + B_VAL_DIM

N_EXPERTS = 32
TOP_K = 4
D_FF = D_MODEL
SWIGLU_LIMIT = 7.0
SWIGLU_ALPHA = 1.702
MOE_BLOCK = 256

kernel_name = "hybrid_dsa_gla_moe_deepnorm"


def layer_norm(x, g, b):
    xf = x.astype(jnp.float32)
    mu = jnp.mean(xf, axis=-1, keepdims=True)
    xc = xf - mu
    var = jnp.mean(xc * xc, axis=-1, keepdims=True)
    y = xc * lax.rsqrt(var + LN_EPS) * g.astype(jnp.float32) + b.astype(jnp.float32)
    return y.astype(x.dtype)


def rms_norm(x, g):
    xf = x.astype(jnp.float32)
    y = xf * lax.rsqrt(jnp.mean(xf * xf, axis=-1, keepdims=True) + RMS_EPS)
    return (y * g.astype(jnp.float32)).astype(x.dtype)


def dsa_mixer(x, w_in, kv_norm, w_uk, w_uv, w_out):
    Bsz, L, _ = x.shape
    proj = x @ w_in
    o1 = A_HEADS * A_QK_DIM
    o2 = o1 + A_LATENT
    o3 = o2 + IDX_HEADS * IDX_DIM
    o4 = o3 + IDX_DIM
    q, c, qi, ki, wi = jnp.split(proj, [o1, o2, o3, o4], axis=-1)
    q = q.reshape(Bsz, L, A_HEADS, A_QK_DIM)
    c = rms_norm(c, kv_norm)
    qi = qi.reshape(Bsz, L, IDX_HEADS, IDX_DIM)
    wi = wi * (IDX_HEADS ** -0.5)
    topk = min(IDX_TOPK_MAX, L // 4)
    nblk = L // Q_BLOCK
    key_pos = jnp.arange(L)

    def to_blocks(t):
        return jnp.moveaxis(t.reshape((Bsz, nblk, Q_BLOCK) + t.shape[2:]), 1, 0)

    def block_fn(args):
        q_b, qi_b, wi_b, blk = args
        qpos = blk * Q_BLOCK + jnp.arange(Q_BLOCK)
        s = jnp.einsum('bqhd,bsd->bqhs', qi_b, ki).astype(jnp.float32) * (IDX_DIM ** -0.5)
        score = jnp.einsum('bqh,bqhs->bqs', wi_b.astype(jnp.float32), jax.nn.relu(s))
        causal = key_pos[None, :] <= qpos[:, None]
        score = jnp.where(causal[None], score, -jnp.inf)
        _, idx = lax.top_k(score, topk)
        valid = idx <= qpos[None, :, None]
        c_sel = jax.vmap(lambda cb, ib: cb[ib])(c, idx)
        q_lat = jnp.einsum('bqhd,hdc->bqhc', q_b, w_uk)
        logits = jnp.einsum('bqhc,bqkc->bqhk', q_lat, c_sel).astype(jnp.float32) * (A_QK_DIM ** -0.5)
        logits = jnp.where(valid[:, :, None, :], logits, -jnp.inf)
        p = jax.nn.softmax(logits, axis=-1).astype(c_sel.dtype)
        return jnp.einsum('bqhk,bqkc->bqhc', p, c_sel)

    o = lax.map(block_fn, (to_blocks(q), to_blocks(qi), to_blocks(wi), jnp.arange(nblk)))
    o = jnp.moveaxis(o, 0, 1).reshape(Bsz, L, A_HEADS, A_LATENT)
    o = jnp.einsum('blhc,hcv->blhv', o, w_uv).reshape(Bsz, L, A_HEADS * A_V_DIM)
    return (o @ w_out).astype(x.dtype)


def gla_mixer(x, w_in, w_g2, g_bias, norm_g, w_out):
    Bsz, L, _ = x.shape
    proj = x @ w_in
    q, k, v, g_lr, r = jnp.split(
        proj, [B_KEY_DIM, 2 * B_KEY_DIM, 2 * B_KEY_DIM + B_VAL_DIM,
               2 * B_KEY_DIM + B_VAL_DIM + B_GATE_RANK], axis=-1)
    gate_logit = (g_lr @ w_g2 + g_bias).astype(jnp.float32)
    log_a = jax.nn.log_sigmoid(gate_logit) / B_GATE_TAU
    nC = L // B_CHUNK

    def to_chunks(t, hd):
        t = t.astype(jnp.float32).reshape(Bsz, nC, B_CHUNK, B_HEADS, hd)
        return jnp.transpose(t, (1, 0, 3, 2, 4))

    qc = to_chunks(q, B_KH) * (B_KH ** -0.5)
    kc = to_chunks(k, B_KH)
    vc = to_chunks(v, B_VH)
    ac = to_chunks(log_a, B_KH)
    tril = jnp.tril(jnp.ones((B_CHUNK, B_CHUNK), dtype=bool))

    def step(S, inp):
        qb, kb, vb, ab = inp
        bcum = jnp.cumsum(ab, axis=2)
        o_inter = jnp.einsum('bhid,bhde->bhie', qb * jnp.exp(bcum), S)
        diff = bcum[:, :, :, None, :] - bcum[:, :, None, :, :]
        decay = jnp.exp(jnp.where(tril[None, None, :, :, None], diff, -jnp.inf))
        A = jnp.einsum('bhijd,bhjd->bhij', qb[:, :, :, None, :] * decay, kb)
        o_intra = jnp.einsum('bhij,bhje->bhie', A, vb)
        b_last = bcum[:, :, -1:, :]
        S_new = jnp.exp(b_last[:, :, 0, :])[..., None] * S + jnp.einsum(
            'bhjd,bhje->bhde', kb * jnp.exp(b_last - bcum), vb)
        return S_new, o_inter + o_intra

    S0 = jnp.zeros((Bsz, B_HEADS, B_KH, B_VH), jnp.float32)
    _, o = lax.scan(step, S0, (qc, kc, vc, ac))
    o = jnp.transpose(o, (1, 0, 3, 2, 4)).reshape(Bsz, L, B_HEADS, B_VH)
    o = rms_norm(o, norm_g.reshape(B_HEADS, B_VH))
    o = o.reshape(Bsz, L, B_VAL_DIM) * jax.nn.silu(r.astype(jnp.float32))
    return (o.astype(x.dtype) @ w_out).astype(x.dtype)


def moe_ffn(x, w_router, b_router, w1, b1, w2, b2):
    Bsz, L, D = x.shape
    N = Bsz * L
    xt = x.reshape(N, D)
    logits = (xt @ w_router + b_router).astype(jnp.float32)
    top_vals, top_idx = lax.top_k(logits, TOP_K)
    gates = jax.nn.softmax(top_vals, axis=-1)
    NK = N * TOP_K
    flat_e = top_idx.reshape(-1)
    flat_tok = jnp.repeat(jnp.arange(N, dtype=jnp.int32), TOP_K)
    flat_gate = gates.reshape(-1)
    order = jnp.argsort(flat_e)
    se, stok, sgate = flat_e[order], flat_tok[order], flat_gate[order]
    counts = jnp.bincount(flat_e, length=N_EXPERTS)
    padded = ((counts + MOE_BLOCK - 1) // MOE_BLOCK) * MOE_BLOCK
    start = jnp.cumsum(counts) - counts
    pend = jnp.cumsum(padded)
    pstart = pend - padded
    dest = pstart[se] + (jnp.arange(NK) - start[se])
    n_blocks = -(-NK // MOE_BLOCK) + N_EXPERTS
    P = n_blocks * MOE_BLOCK
    row_tok = jnp.zeros((P,), jnp.int32).at[dest].set(stok)
    row_gate = jnp.zeros((P,), jnp.float32).at[dest].set(sgate)
    blk_e = jnp.minimum(jnp.searchsorted(pend, jnp.arange(n_blocks) * MOE_BLOCK, side='right'),
                        N_EXPERTS - 1)
    xs = xt[row_tok].reshape(n_blocks, MOE_BLOCK, D)

    def expert_block(args):
        xb, e = args
        h = xb @ w1[e] + b1[e]
        g, u = h[:, :D_FF], h[:, D_FF:]
        g = jnp.minimum(g, SWIGLU_LIMIT)
        u = jnp.clip(u, -SWIGLU_LIMIT, SWIGLU_LIMIT)
        glu = g * jax.nn.sigmoid(g * SWIGLU_ALPHA)
        return ((u + 1.0) * glu) @ w2[e] + b2[e]

    ys = lax.map(expert_block, (xs, blk_e)).reshape(P, D)
    y = jnp.zeros((N, D), x.dtype).at[row_tok].add((ys * row_gate[:, None].astype(ys.dtype)).astype(x.dtype))
    return y.reshape(Bsz, L, D)


def setup_inputs(seed: int = 0) -> dict:
    key = jax.random.key(seed)
    ks = jax.random.split(key, 24)
    f32 = jnp.float32
    nrm = lambda k, shape, s: jax.random.normal(k, shape, f32) * s
    D = D_MODEL
    LA, LB = N_DSA_LAYERS, N_GLA_LAYERS
    return {
        "x": nrm(ks[0], (BATCH, SEQ, D), 1.0),
        "a_w_in": nrm(ks[1], (LA, D, D_IN_A), D ** -0.5),
        "a_kv_norm": 1.0 + nrm(ks[2], (LA, A_LATENT), 0.01),
        "a_w_uk": nrm(ks[3], (LA, A_HEADS, A_QK_DIM, A_LATENT), A_LATENT ** -0.5),
        "a_w_uv": nrm(ks[4], (LA, A_HEADS, A_LATENT, A_V_DIM), A_LATENT ** -0.5 * DN_BETA),
        "a_w_out": nrm(ks[5], (LA, A_HEADS * A_V_DIM, D), (A_HEADS * A_V_DIM) ** -0.5 * DN_BETA),
        "b_w_in": nrm(ks[6], (LB, D, D_IN_B), D ** -0.5),
        "b_w_g2": nrm(ks[7], (LB, B_GATE_RANK, B_KEY_DIM), B_GATE_RANK ** -0.5),
        "b_g_bias": nrm(ks[8], (LB, B_KEY_DIM), 0.1),
        "b_norm": 1.0 + nrm(ks[9], (LB, B_VAL_DIM), 0.01),
        "b_w_out": nrm(ks[10], (LB, B_VAL_DIM, D), B_VAL_DIM ** -0.5 * DN_BETA),
        "m_w_router": nrm(ks[11], (DEPTH, D, N_EXPERTS), D ** -0.5),
        "m_b_router": nrm(ks[12], (DEPTH, N_EXPERTS), 0.01),
        "m_w1": nrm(ks[13], (DEPTH, N_EXPERTS, D, 2 * D_FF), D ** -0.5),
        "m_b1": nrm(ks[14], (DEPTH, N_EXPERTS, 2 * D_FF), 0.01),
        "m_w2": nrm(ks[15], (DEPTH, N_EXPERTS, D_FF, D), D_FF ** -0.5 * DN_BETA),
        "m_b2": nrm(ks[16], (DEPTH, N_EXPERTS, D), 0.01),
        "ln1_g": 1.0 + nrm(ks[17], (DEPTH, D), 0.01),
        "ln1_b": nrm(ks[18], (DEPTH, D), 0.01),
        "ln2_g": 1.0 + nrm(ks[19], (DEPTH, D), 0.01),
        "ln2_b": nrm(ks[20], (DEPTH, D), 0.01),
    }


def reference(x, a_w_in, a_kv_norm, a_w_uk, a_w_uv, a_w_out,
              b_w_in, b_w_g2, b_g_bias, b_norm, b_w_out,
              m_w_router, m_b_router, m_w1, m_b1, m_w2, m_b2,
              ln1_g, ln1_b, ln2_g, ln2_b):
    for i in range(DEPTH):
        j = i // N_MIXERS
        if i % N_MIXERS == 0:
            h = dsa_mixer(x, a_w_in[j], a_kv_norm[j], a_w_uk[j], a_w_uv[j], a_w_out[j])
        else:
            h = gla_mixer(x, b_w_in[j], b_w_g2[j], b_g_bias[j], b_norm[j], b_w_out[j])
        x = layer_norm(DN_ALPHA * x + h, ln1_g[i], ln1_b[i])
        f = moe_ffn(x, m_w_router[i], m_b_router[i], m_w1[i], m_b1[i], m_w2[i], m_b2[i])
        x = layer_norm(DN_ALPHA * x + f, ln2_g[i], ln2_b[i])
    return x
```

```python
import functools

import jax
import jax.numpy as jnp
from jax import lax
from jax.experimental import pallas as pl
from jax.experimental.pallas import tpu as pltpu

F32 = jnp.float32
BF16 = jnp.bfloat16

DEPTH = 2
DN_ALPHA = (2.0 * DEPTH) ** 0.25
LN_EPS = 1e-5
RMS_EPS = 1e-6

A_HEADS = 16
A_QK_DIM = 64
A_V_DIM = 64
A_LATENT = 256
IDX_HEADS = 8
IDX_DIM = 64
IDX_TOPK_MAX = 256
A_QBLK = 128
A_KBLK = 512
A_HGRP = 4

B_HEADS = 4
B_GATE_RANK = 16
B_GATE_TAU = 16.0
B_CHUNK = 64
B_SUB = 16
B_TBLK = 512

N_EXPERTS = 32
TOP_K = 4
SWIGLU_LIMIT = 7.0
SWIGLU_ALPHA = 1.702
MOE_BLOCK = 256
ROW_TILE = 512
MOE_TOK_TILE = 256

VMEM_LIMIT = 56 * 1024 * 1024
NEG_BIG = -1e30


def _cparams(sem):
    return pltpu.CompilerParams(dimension_semantics=sem, vmem_limit_bytes=VMEM_LIMIT)


def _dot(a, b):
    return jnp.dot(a, b, preferred_element_type=F32)


def _dot_nt(a, b):
    return lax.dot_general(a, b, (((1,), (1,)), ((), ())), preferred_element_type=F32)


def _dot_tn(a, b):
    return lax.dot_general(a, b, (((0,), (0,)), ((), ())), preferred_element_type=F32)


def _layer_norm(z, g, b):
    mu = jnp.mean(z, axis=-1, keepdims=True)
    zc = z - mu
    var = jnp.mean(zc * zc, axis=-1, keepdims=True)
    return zc * lax.rsqrt(var + LN_EPS) * g + b


def _dsa_proj_kernel(x_ref, wq_ref, wc_ref, wqi_ref, wkit_ref, wwi_ref, kvn_ref, wuk_ref,
                     qlat_ref, c_ref, ct_ref, qi_ref, kit_ref, wi_ref):
    xb = x_ref[...].astype(BF16)
    q = _dot(xb, wq_ref[...])
    for h in range(A_HEADS):
        qh = q[:, h * A_QK_DIM:(h + 1) * A_QK_DIM].astype(BF16)
        ql = _dot(qh, wuk_ref[h]) * (A_QK_DIM ** -0.5)
        qlat_ref[h] = ql.astype(BF16)
    c = _dot(xb, wc_ref[...])
    c = c * lax.rsqrt(jnp.mean(c * c, axis=-1, keepdims=True) + RMS_EPS) * kvn_ref[...]
    c_ref[...] = c.astype(BF16)
    ct_ref[...] = c.T.astype(BF16)
    qi = _dot(xb, wqi_ref[...]) * (IDX_DIM ** -0.5)
    for h in range(IDX_HEADS):
        qi_ref[h] = qi[:, h * IDX_DIM:(h + 1) * IDX_DIM].astype(BF16)
    kit_ref[...] = _dot_nt(wkit_ref[...], xb).astype(BF16)
    wi_ref[...] = _dot(xb, wwi_ref[...]) * (IDX_HEADS ** -0.5)


def _dsa_proj(x, w_in, kv_norm, w_uk):
    Bsz, L, D = x.shape
    T = A_KBLK
    nkb = L // T
    o1 = A_HEADS * A_QK_DIM
    o2 = o1 + A_LATENT
    o3 = o2 + IDX_HEADS * IDX_DIM
    o4 = o3 + IDX_DIM
    wb = w_in.astype(BF16)
    wq, wc, wqi, wki, wwi = wb[:, :o1], wb[:, o1:o2], wb[:, o2:o3], wb[:, o3:o4], wb[:, o4:]
    full = lambda shape: pl.BlockSpec(shape, lambda b, t: (0,) * len(shape))
    return pl.pallas_call(
        _dsa_proj_kernel,
        grid=(Bsz, nkb),
        in_specs=[
            pl.BlockSpec((None, T, D), lambda b, t: (b, t, 0)),
            full(wq.shape), full(wc.shape), full(wqi.shape), full((IDX_DIM, D)), full(wwi.shape),
            full((1, A_LATENT)), full(w_uk.shape),
        ],
        out_specs=[
            pl.BlockSpec((None, A_HEADS, T, A_LATENT), lambda b, t: (b, 0, t, 0)),
            pl.BlockSpec((None, T, A_LATENT), lambda b, t: (b, t, 0)),
            pl.BlockSpec((None, None, A_LATENT, T), lambda b, t: (b, t, 0, 0)),
            pl.BlockSpec((None, IDX_HEADS, T, IDX_DIM), lambda b, t: (b, 0, t, 0)),
            pl.BlockSpec((None, None, IDX_DIM, T), lambda b, t: (b, t, 0, 0)),
            pl.BlockSpec((None, T, IDX_HEADS), lambda b, t: (b, t, 0)),
        ],
        out_shape=[
            jax.ShapeDtypeStruct((Bsz, A_HEADS, L, A_LATENT), BF16),
            jax.ShapeDtypeStruct((Bsz, L, A_LATENT), BF16),
            jax.ShapeDtypeStruct((Bsz, nkb, A_LATENT, T), BF16),
            jax.ShapeDtypeStruct((Bsz, IDX_HEADS, L, IDX_DIM), BF16),
            jax.ShapeDtypeStruct((Bsz, nkb, IDX_DIM, T), BF16),
            jax.ShapeDtypeStruct((Bsz, L, IDX_HEADS), F32),
        ],
        compiler_params=_cparams(("parallel", "parallel")),
        name="dsa_proj",
    )(x, wq, wc, wqi, wki.T, wwi, kv_norm.reshape(1, A_LATENT), w_uk.astype(BF16))


def _ordered_bits_to_float(u):
    o = u ^ jnp.int32(-2 ** 31)
    bits = jnp.where(o >= 0, o, o ^ jnp.int32(2 ** 31 - 1))
    return lax.bitcast_convert_type(bits, F32)


def _dsa_attn_kernel(qi_ref, wi_ref, kit_ref, qlat_ref, ct_ref, c_ref, wuv_ref, o_ref,
                     sc_ref, wb_ref, m_ref, l_ref, acc_ref, *, topk):
    QB, KB = A_QBLK, A_KBLK
    G = A_HGRP
    GR = G * QB
    qb = pl.program_id(1)
    q0 = qb * QB
    nvb = (q0 + QB + KB - 1) // KB
    qpos = q0 + lax.broadcasted_iota(jnp.int32, (QB, 1), 0)
    lane = lax.broadcasted_iota(jnp.int32, (QB, KB), 1)

    wi = wi_ref[...]
    for h in range(IDX_HEADS):
        wb_ref[h] = jnp.broadcast_to(wi[:, h:h + 1], (QB, 128))
    qi = qi_ref[...].reshape(IDX_HEADS * QB, IDX_DIM)

    def score_body(kb, carry):
        s = _dot(qi, kit_ref[kb])
        s = jnp.maximum(s, 0.0).reshape(IDX_HEADS, QB, KB)
        parts = []
        for cch in range(KB // 128):
            parts.append(jnp.sum(s[:, :, cch * 128:(cch + 1) * 128] * wb_ref[...], axis=0))
        sc = jnp.concatenate(parts, axis=1)
        causal = (kb * KB + lane) <= qpos
        sc_ref[kb] = jnp.where(causal, sc, -jnp.inf)
        return carry

    lax.fori_loop(0, nvb, score_body, 0)

    def count_ge(cand):
        def cbody(kb, acc):
            ge = jnp.where(sc_ref[kb] >= cand, 1.0, 0.0)
            for cch in range(KB // 128):
                acc = acc + ge[:, cch * 128:(cch + 1) * 128]
            return acc
        acc = lax.fori_loop(0, nvb, cbody, jnp.zeros((QB, 128), F32))
        return jnp.sum(acc, axis=1, keepdims=True)

    def bis_body(i, u):
        trial = u | jnp.left_shift(jnp.int32(1), 31 - i)
        cnt = count_ge(_ordered_bits_to_float(trial))
        return jnp.where(cnt >= float(topk), trial, u)

    u = lax.fori_loop(0, 32, bis_body, jnp.zeros((QB, 1), jnp.int32))
    thr = jnp.where(qpos < topk, -jnp.inf, _ordered_bits_to_float(u))

    m_ref[...] = jnp.full(m_ref.shape, NEG_BIG, F32)
    l_ref[...] = jnp.zeros(l_ref.shape, F32)
    acc_ref[...] = jnp.zeros(acc_ref.shape, F32)

    def attn_body(kb, carry):
        sel = (sc_ref[kb] >= thr) & ((kb * KB + lane) <= qpos)
        bias = jnp.where(sel, 0.0, NEG_BIG)
        ct = ct_ref[kb]
        cb = c_ref[kb]

        def grp_body(g, carry2):
            r0 = pl.multiple_of(g * GR, GR)
            ql = qlat_ref[pl.ds(g * G, G)].reshape(GR, A_LATENT)
            s = _dot(ql, ct).reshape(G, QB, KB) + bias[None]
            s = s.reshape(GR, KB)
            m_prev = m_ref[pl.ds(r0, GR), :]
            m_new = jnp.maximum(m_prev, jnp.max(s, axis=1, keepdims=True))
            alpha = jnp.exp(m_prev - m_new)
            p = jnp.exp(s - m_new)
            l_ref[pl.ds(r0, GR), :] = alpha * l_ref[pl.ds(r0, GR), :] + jnp.sum(p, axis=1, keepdims=True)
            acc_ref[pl.ds(r0, GR), :] = alpha * acc_ref[pl.ds(r0, GR), :] + _dot(p.astype(BF16), cb)
            m_ref[pl.ds(r0, GR), :] = m_new
            return carry2

        lax.fori_loop(0, A_HEADS // G, grp_body, 0)
        return carry

    lax.fori_loop(0, nvb, attn_body, 0)

    for h in range(A_HEADS):
        rows = slice(h * QB, (h + 1) * QB)
        oh = acc_ref[rows, :] / l_ref[rows, :]
        ov = _dot(oh.astype(BF16), wuv_ref[h])
        o_ref[:, h * A_V_DIM:(h + 1) * A_V_DIM] = ov.astype(BF16)


def _dsa_attention(qi, wi, kit, qlat, ct, c, w_uv):
    Bsz, _, L, _ = qlat.shape
    QB, KB = A_QBLK, A_KBLK
    nkb = L // KB
    topk = min(IDX_TOPK_MAX, L // 4)
    c4 = c.reshape(Bsz, nkb, KB, A_LATENT)
    kern = functools.partial(_dsa_attn_kernel, topk=topk)
    return pl.pallas_call(
        kern,
        grid=(Bsz, L // QB),
        in_specs=[
            pl.BlockSpec((None, IDX_HEADS, QB, IDX_DIM), lambda b, q: (b, 0, q, 0)),
            pl.BlockSpec((None, QB, IDX_HEADS), lambda b, q: (b, q, 0)),
            pl.BlockSpec((None, nkb, IDX_DIM, KB), lambda b, q: (b, 0, 0, 0)),
            pl.BlockSpec((None, A_HEADS, QB, A_LATENT), lambda b, q: (b, 0, q, 0)),
            pl.BlockSpec((None, nkb, A_LATENT, KB), lambda b, q: (b, 0, 0, 0)),
            pl.BlockSpec((None, nkb, KB, A_LATENT), lambda b, q: (b, 0, 0, 0)),
            pl.BlockSpec((A_HEADS, A_LATENT, A_V_DIM), lambda b, q: (0, 0, 0)),
        ],
        out_specs=pl.BlockSpec((None, QB, A_HEADS * A_V_DIM), lambda b, q: (b, q, 0)),
        out_shape=jax.ShapeDtypeStruct((Bsz, L, A_HEADS * A_V_DIM), BF16),
        scratch_shapes=[
            pltpu.VMEM((nkb, QB, KB), F32),
            pltpu.VMEM((IDX_HEADS, QB, 128), F32),
            pltpu.VMEM((A_HEADS * QB, 1), F32),
            pltpu.VMEM((A_HEADS * QB, 1), F32),
            pltpu.VMEM((A_HEADS * QB, A_LATENT), F32),
        ],
        compiler_params=_cparams(("parallel", "parallel")),
        name="dsa_attn",
    )(qi, wi, kit, qlat, ct, c4, w_uv.astype(BF16))


def _outproj_ln_kernel(o_ref, w_ref, x_ref, g_ref, b_ref, y_ref):
    h = _dot(o_ref[...], w_ref[...])
    y_ref[...] = _layer_norm(DN_ALPHA * x_ref[...] + h, g_ref[...], b_ref[...])


def _outproj_ln(o, w_out, x, g, b):
    N, K = o.shape
    D = x.shape[1]
    T = ROW_TILE
    return pl.pallas_call(
        _outproj_ln_kernel,
        grid=(N // T,),
        in_specs=[
            pl.BlockSpec((T, K), lambda i: (i, 0)),
            pl.BlockSpec((K, D), lambda i: (0, 0)),
            pl.BlockSpec((T, D), lambda i: (i, 0)),
            pl.BlockSpec((1, D), lambda i: (0, 0)),
            pl.BlockSpec((1, D), lambda i: (0, 0)),
        ],
        out_specs=pl.BlockSpec((T, D), lambda i: (i, 0)),
        out_shape=jax.ShapeDtypeStruct((N, D), F32),
        compiler_params=_cparams(("parallel",)),
        name="outproj_ln",
    )(o, w_out.astype(BF16), x, g.reshape(1, D), b.reshape(1, D))


def _gla_proj_kernel(x_ref, wq_ref, wk_ref, wv_ref, wg_ref, wr_ref, wg2_ref, gb_ref,
                     q_ref, k_ref, v_ref, la_ref, r_ref, *, kh):
    xb = x_ref[...].astype(BF16)
    q_ref[...] = _dot(xb, wq_ref[...]) * (kh ** -0.5)
    k_ref[...] = _dot(xb, wk_ref[...])
    v_ref[...] = _dot(xb, wv_ref[...]).astype(BF16)
    r_ref[...] = _dot(xb, wr_ref[...])
    glr = _dot(xb, wg_ref[...])
    z = _dot(glr.astype(BF16), wg2_ref[...]) + gb_ref[...]
    log_sig = jnp.minimum(z, 0.0) - jnp.log(1.0 + jnp.exp(-jnp.abs(z)))
    la_ref[...] = log_sig / B_GATE_TAU


def _gla_proj(x, w_in, w_g2, g_bias):
    N, D = x.shape
    DK = w_g2.shape[1]
    DV = D
    T = ROW_TILE
    wb = w_in.astype(BF16)
    o1, o2, o3, o4 = DK, 2 * DK, 2 * DK + DV, 2 * DK + DV + B_GATE_RANK
    wq, wk, wv, wg, wr = wb[:, :o1], wb[:, o1:o2], wb[:, o2:o3], wb[:, o3:o4], wb[:, o4:]
    full = lambda shape: pl.BlockSpec(shape, lambda i: (0,) * len(shape))
    row = lambda w: pl.BlockSpec((T, w), lambda i: (i, 0))
    kern = functools.partial(_gla_proj_kernel, kh=DK // B_HEADS)
    return pl.pallas_call(
        kern,
        grid=(N // T,),
        in_specs=[row(D), full(wq.shape), full(wk.shape), full(wv.shape), full(wg.shape),
                  full(wr.shape), full(w_g2.shape), full((1, DK))],
        out_specs=[row(DK), row(DK), row(DV), row(DK), row(DV)],
        out_shape=[
            jax.ShapeDtypeStruct((N, DK), F32),
            jax.ShapeDtypeStruct((N, DK), F32),
            jax.ShapeDtypeStruct((N, DV), BF16),
            jax.ShapeDtypeStruct((N, DK), F32),
            jax.ShapeDtypeStruct((N, DV), F32),
        ],
        compiler_params=_cparams(("parallel",)),
        name="gla_proj",
    )(x, wq, wk, wv, wg, wr, w_g2.astype(BF16), g_bias.reshape(1, DK))


def _split3(a):
    hi = a.astype(BF16)
    r1 = a - hi.astype(F32)
    mid = r1.astype(BF16)
    lo = (r1 - mid.astype(F32)).astype(BF16)
    return hi, mid, lo


def _gla_kernel(q_ref, k_ref, v_ref, la_ref, r_ref, ng_ref, o_ref, st_ref, a_ref):
    C, SB = B_CHUNK, B_SUB
    nsub = C // SB

    @pl.when(pl.program_id(2) == 0)
    def _():
        st_ref[...] = jnp.zeros(st_ref.shape, F32)

    ri = lax.broadcasted_iota(jnp.int32, (C, C), 0)
    ci = lax.broadcasted_iota(jnp.int32, (C, C), 1)
    tril = jnp.where(ci <= ri, 1.0, 0.0).astype(BF16)
    sub_r = lax.broadcasted_iota(jnp.int32, (SB, 1), 0)
    sub_c = lax.broadcasted_iota(jnp.int32, (1, SB), 1)

    def chunk_body(ch, carry):
        rows = pl.ds(pl.multiple_of(ch * C, C), C)
        q = q_ref[rows, :]
        k = k_ref[rows, :]
        v = v_ref[rows, :]
        la = la_ref[rows, :]
        hi, mid, lo = _split3(la)
        b = _dot(tril, hi) + _dot(tril, mid) + _dot(tril, lo)
        st = st_ref[...]
        o = _dot_nt((q * jnp.exp(b)).astype(BF16), st.astype(BF16))

        a_ref[...] = jnp.zeros(a_ref.shape, F32)
        for I in range(nsub):
            r0 = I * SB
            bI = b[r0:r0 + SB, :]
            qI = q[r0:r0 + SB, :]
            kI = k[r0:r0 + SB, :]
            if I > 0:
                ref_lvl = b[r0 - 1:r0, :]
                qs = (qI * jnp.exp(bI - ref_lvl)).astype(BF16)
                ks = (k[:r0, :] * jnp.exp(ref_lvl - b[:r0, :])).astype(BF16)
                a_ref[r0:r0 + SB, :r0] = _dot_nt(qs, ks)
            diag = jnp.zeros((SB, SB), F32)
            for j in range(SB):
                dlt = jnp.where(sub_r >= j, bI - bI[j:j + 1, :], -jnp.inf)
                col = jnp.sum(qI * kI[j:j + 1, :] * jnp.exp(dlt), axis=1, keepdims=True)
                diag = diag + col * jnp.where(sub_c == j, 1.0, 0.0)
            a_ref[r0:r0 + SB, r0:r0 + SB] = diag
        o = o + _dot(a_ref[...].astype(BF16), v)

        b_last = b[C - 1:C, :]
        kd = (k * jnp.exp(b_last - b)).astype(BF16)
        st_ref[...] = st * jnp.exp(b_last) + _dot_tn(v, kd)

        o = o * lax.rsqrt(jnp.mean(o * o, axis=-1, keepdims=True) + RMS_EPS) * ng_ref[...]
        r = r_ref[rows, :]
        o_ref[rows, :] = (o * (r / (1.0 + jnp.exp(-r)))).astype(BF16)
        return carry

    lax.fori_loop(0, q_ref.shape[0] // C, chunk_body, 0)


def _gla_core(q, k, v, la, r, norm_g):
    Bsz, L, DK = q.shape
    DV = v.shape[2]
    kh, vh = DK // B_HEADS, DV // B_HEADS
    T = B_TBLK
    kspec = pl.BlockSpec((None, T, kh), lambda b, h, t: (b, t, h))
    vspec = pl.BlockSpec((None, T, vh), lambda b, h, t: (b, t, h))
    return pl.pallas_call(
        _gla_kernel,
        grid=(Bsz, B_HEADS, L // T),
        in_specs=[kspec, kspec, vspec, kspec, vspec,
                  pl.BlockSpec((1, vh), lambda b, h, t: (0, h))],
        out_specs=vspec,
        out_shape=jax.ShapeDtypeStruct((Bsz, L, DV), BF16),
        scratch_shapes=[pltpu.VMEM((vh, kh), F32), pltpu.VMEM((B_CHUNK, B_CHUNK), F32)],
        compiler_params=_cparams(("parallel", "parallel", "arbitrary")),
        name="gla_core",
    )(q, k, v, la, r, norm_g.reshape(1, DV))


def _router_kernel(x_ref, w_ref, b_ref, idx_ref, gate_ref, rank_ref, cnt_ref, carry_ref):
    T = x_ref.shape[0]
    E = N_EXPERTS

    @pl.when(pl.program_id(0) == 0)
    def _():
        carry_ref[...] = jnp.zeros(carry_ref.shape, F32)

    x = x_ref[...]
    w = w_ref[...]
    xh = x.astype(BF16)
    xl = (x - xh.astype(F32)).astype(BF16)
    wh = w.astype(BF16)
    wl = (w - wh.astype(F32)).astype(BF16)
    logits = _dot(xh, wh) + (_dot(xh, wl) + _dot(xl, wh)) + b_ref[...]

    eidx = lax.broadcasted_iota(jnp.int32, (T, E), 1)
    work = logits
    vals, idxs = [], []
    for _ in range(TOP_K):
        mx = jnp.max(work, axis=1, keepdims=True)
        ix = jnp.min(jnp.where(work == mx, eidx, E), axis=1, keepdims=True)
        vals.append(mx)
        idxs.append(ix)
        work = jnp.where(eidx == ix, -jnp.inf, work)
    exps = [jnp.exp(vv - vals[0]) for vv in vals]
    den = exps[0] + exps[1] + exps[2] + exps[3]

    onehot = jnp.zeros((T, E), F32)
    for ix in idxs:
        onehot = onehot + jnp.where(eidx == ix, 1.0, 0.0)
    ri = lax.broadcasted_iota(jnp.int32, (T, T), 0)
    ci = lax.broadcasted_iota(jnp.int32, (T, T), 1)
    strict = jnp.where(ci < ri, 1.0, 0.0).astype(BF16)
    prefix = _dot(strict, onehot.astype(BF16)) + carry_ref[...]
    carry_ref[...] = carry_ref[...] + jnp.sum(onehot, axis=0, keepdims=True)
    cnt_ref[...] = carry_ref[...]

    for j in range(TOP_K):
        idx_ref[:, j:j + 1] = idxs[j]
        gate_ref[:, j:j + 1] = exps[j] / den
        rk = jnp.sum(jnp.where(eidx == idxs[j], prefix, 0.0), axis=1, keepdims=True)
        rank_ref[:, j:j + 1] = rk.astype(jnp.int32)


def _router(xt, w_router, b_router):
    N, D = xt.shape
    T = MOE_TOK_TILE
    tok = pl.BlockSpec((T, TOP_K), lambda i: (i, 0))
    return pl.pallas_call(
        _router_kernel,
        grid=(N // T,),
        in_specs=[
            pl.BlockSpec((T, D), lambda i: (i, 0)),
            pl.BlockSpec((D, N_EXPERTS), lambda i: (0, 0)),
            pl.BlockSpec((1, N_EXPERTS), lambda i: (0, 0)),
        ],
        out_specs=[tok, tok, tok, pl.BlockSpec((1, N_EXPERTS), lambda i: (0, 0))],
        out_shape=[
            jax.ShapeDtypeStruct((N, TOP_K), jnp.int32),
            jax.ShapeDtypeStruct((N, TOP_K), F32),
            jax.ShapeDtypeStruct((N, TOP_K), jnp.int32),
            jax.ShapeDtypeStruct((1, N_EXPERTS), F32),
        ],
        scratch_shapes=[pltpu.VMEM((1, N_EXPERTS), F32)],
        compiler_params=_cparams(("arbitrary",)),
        name="moe_router",
    )(xt, w_router, b_router.reshape(1, N_EXPERTS))


def _dispatch_kernel(dest_ref, x_ref, xs_in_ref, xs_ref, sem):
    del xs_in_ref
    T = x_ref.shape[0]

    def row_copy(i):
        r = i // TOP_K
        return pltpu.make_async_copy(x_ref.at[pl.ds(r, 1)], xs_ref.at[pl.ds(dest_ref[i], 1)], sem)

    def issue(i, c):
        row_copy(i).start()
        return c

    def drain(i, c):
        row_copy(i).wait()
        return c

    lax.fori_loop(0, T * TOP_K, issue, 0)
    lax.fori_loop(0, T * TOP_K, drain, 0)


def _dispatch(xt, dest_flat, n_rows):
    N, D = xt.shape
    T = MOE_TOK_TILE
    xs0 = jnp.zeros((n_rows, D), F32)
    return pl.pallas_call(
        _dispatch_kernel,
        grid=(N // T,),
        in_specs=[
            pl.BlockSpec((T * TOP_K,), lambda i: (i,), memory_space=pltpu.SMEM),
            pl.BlockSpec((T, D), lambda i: (i, 0)),
            pl.BlockSpec(memory_space=pl.ANY),
        ],
        out_specs=pl.BlockSpec(memory_space=pl.ANY),
        out_shape=jax.ShapeDtypeStruct((n_rows, D), F32),
        scratch_shapes=[pltpu.SemaphoreType.DMA],
        input_output_aliases={2: 0},
        compiler_params=_cparams(("arbitrary",)),
        name="moe_dispatch",
    )(dest_flat, xt, xs0)


def _expert_kernel(blk_e_ref, nused_ref, xs_ref, w1_ref, b1_ref, w2_ref, b2_ref, ys_ref,
                   w1b_ref, w2b_ref):
    i = pl.program_id(0)
    prev = blk_e_ref[jnp.maximum(i - 1, 0)]
    fresh = (i == 0) | (blk_e_ref[i] != prev)

    @pl.when(fresh)
    def _():
        w1b_ref[...] = w1_ref[...].astype(BF16)
        w2b_ref[...] = w2_ref[...].astype(BF16)

    @pl.when(i < nused_ref[0])
    def _():
        F = w2_ref.shape[0]
        h = _dot(xs_ref[...].astype(BF16), w1b_ref[...]) + b1_ref[...]
        g = jnp.minimum(h[:, :F], SWIGLU_LIMIT)
        u = jnp.clip(h[:, F:], -SWIGLU_LIMIT, SWIGLU_LIMIT)
        glu = g / (1.0 + jnp.exp(-SWIGLU_ALPHA * g))
        act = ((u + 1.0) * glu).astype(BF16)
        ys_ref[...] = _dot(act, w2b_ref[...]) + b2_ref[...]

    @pl.when(i >= nused_ref[0])
    def _():
        ys_ref[...] = jnp.zeros(ys_ref.shape, F32)


def _experts(xs, blk_e, n_used, w1, b1, w2, b2):
    P, D = xs.shape
    E, _, F2 = w1.shape
    F = w2.shape[1]
    nb = P // MOE_BLOCK
    grid_spec = pltpu.PrefetchScalarGridSpec(
        num_scalar_prefetch=2,
        grid=(nb,),
        in_specs=[
            pl.BlockSpec((MOE_BLOCK, D), lambda i, be, nu: (i, 0)),
            pl.BlockSpec((None, D, F2), lambda i, be, nu: (be[i], 0, 0)),
            pl.BlockSpec((None, 1, F2), lambda i, be, nu: (be[i], 0, 0)),
            pl.BlockSpec((None, F, D), lambda i, be, nu: (be[i], 0, 0)),
            pl.BlockSpec((None, 1, D), lambda i, be, nu: (be[i], 0, 0)),
        ],
        out_specs=pl.BlockSpec((MOE_BLOCK, D), lambda i, be, nu: (i, 0)),
        scratch_shapes=[pltpu.VMEM((D, F2), BF16), pltpu.VMEM((F, D), BF16)],
    )
    return pl.pallas_call(
        _expert_kernel,
        grid_spec=grid_spec,
        out_shape=jax.ShapeDtypeStruct((P, D), F32),
        compiler_params=_cparams(("arbitrary",)),
        name="moe_experts",
    )(blk_e, n_used, xs, w1, b1.reshape(E, 1, F2), w2, b2.reshape(E, 1, D))


def _combine_ln_kernel(dest_ref, gate_ref, x_ref, ys_ref, g_ref, b_ref, y_ref, buf_ref, sem):
    T = x_ref.shape[0]

    def row_copy(i):
        r = i // TOP_K
        j = i % TOP_K
        return pltpu.make_async_copy(ys_ref.at[pl.ds(dest_ref[i], 1)],
                                     buf_ref.at[j, pl.ds(r, 1)], sem)

    def issue(i, c):
        row_copy(i).start()
        return c

    def drain(i, c):
        row_copy(i).wait()
        return c

    lax.fori_loop(0, T * TOP_K, issue, 0)
    lax.fori_loop(0, T * TOP_K, drain, 0)

    gate = gate_ref[...]
    f = jnp.zeros(x_ref.shape, F32)
    for j in range(TOP_K):
        f = f + buf_ref[j] * gate[:, j:j + 1]
    y_ref[...] = _layer_norm(DN_ALPHA * x_ref[...] + f, g_ref[...], b_ref[...])


def _combine_ln(xt, ys, dest_flat, gates, g, b):
    N, D = xt.shape
    T = MOE_TOK_TILE
    return pl.pallas_call(
        _combine_ln_kernel,
        grid=(N // T,),
        in_specs=[
            pl.BlockSpec((T * TOP_K,), lambda i: (i,), memory_space=pltpu.SMEM),
            pl.BlockSpec((T, TOP_K), lambda i: (i, 0)),
            pl.BlockSpec((T, D), lambda i: (i, 0)),
            pl.BlockSpec(memory_space=pl.ANY),
            pl.BlockSpec((1, D), lambda i: (0, 0)),
            pl.BlockSpec((1, D), lambda i: (0, 0)),
        ],
        out_specs=pl.BlockSpec((T, D), lambda i: (i, 0)),
        out_shape=jax.ShapeDtypeStruct((N, D), F32),
        scratch_shapes=[pltpu.VMEM((TOP_K, T, D), F32), pltpu.SemaphoreType.DMA],
        compiler_params=_cparams(("arbitrary",)),
        name="moe_combine_ln",
    )(dest_flat, gates, xt, ys, g.reshape(1, D), b.reshape(1, D))


def _moe_ln(xt, w_router, b_router, w1, b1, w2, b2, g, b):
    N, D = xt.shape
    idx, gates, rank, counts = _router(xt, w_router, b_router)
    counts = counts.reshape(N_EXPERTS).astype(jnp.int32)
    padded = ((counts + MOE_BLOCK - 1) // MOE_BLOCK) * MOE_BLOCK
    pend = jnp.cumsum(padded)
    pstart = pend - padded
    n_blocks = -(-(N * TOP_K) // MOE_BLOCK) + N_EXPERTS
    dest = (pstart[idx] + rank).reshape(N * TOP_K)
    blk_e = jnp.minimum(
        jnp.searchsorted(pend, jnp.arange(n_blocks, dtype=jnp.int32) * MOE_BLOCK, side="right"),
        N_EXPERTS - 1).astype(jnp.int32)
    n_used = (pend[-1:] // MOE_BLOCK).astype(jnp.int32)
    xs = _dispatch(xt, dest, n_blocks * MOE_BLOCK)
    ys = _experts(xs, blk_e, n_used, w1, b1, w2, b2)
    return _combine_ln(xt, ys, dest, gates, g, b)


def _dsa_layer(x, w_in, kv_norm, w_uk, w_uv, w_out, g, b):
    Bsz, L, D = x.shape
    qlat, c, ct, qi, kit, wi = _dsa_proj(x, w_in, kv_norm, w_uk)
    o = _dsa_attention(qi, wi, kit, qlat, ct, c, w_uv)
    return _outproj_ln(o.reshape(Bsz * L, -1), w_out, x.reshape(Bsz * L, D), g, b)


def _gla_layer(x, w_in, w_g2, g_bias, norm_g, w_out, g, b):
    Bsz, L, D = x.shape
    xt = x.reshape(Bsz * L, D)
    q, k, v, la, r = _gla_proj(xt, w_in, w_g2, g_bias)
    sh = lambda t: t.reshape(Bsz, L, t.shape[-1])
    o = _gla_core(sh(q), sh(k), sh(v), sh(la), sh(r), norm_g)
    return _outproj_ln(o.reshape(Bsz * L, -1), w_out, xt, g, b)


def kernel(x, a_w_in, a_kv_norm, a_w_uk, a_w_uv, a_w_out, b_w_in, b_w_g2, b_g_bias, b_norm, b_w_out, m_w_router, m_b_router, m_w1, m_b1, m_w2, m_b2, ln1_g, ln1_b, ln2_g, ln2_b):
    Bsz, L, D = x.shape
    for i in range(DEPTH):
        j = i // 2
        if i % 2 == 0:
            xt = _dsa_layer(x, a_w_in[j], a_kv_norm[j], a_w_uk[j], a_w_uv[j], a_w_out[j],
                            ln1_g[i], ln1_b[i])
        else:
            xt = _gla_layer(x, b_w_in[j], b_w_g2[j], b_g_bias[j], b_norm[j], b_w_out[j],
                            ln1_g[i], ln1_b[i])
        xt = _moe_ln(xt, m_w_router[i], m_b_router[i], m_w1[i], m_b1[i], m_w2[i], m_b2[i],
                     ln2_g[i], ln2_b[i])
        x = xt.reshape(Bsz, L, D)
    return x
```

```python
import functools

import jax
import jax.numpy as jnp
from jax import lax
from jax.experimental import pallas as pl
from jax.experimental.pallas import tpu as pltpu

F32 = jnp.float32
BF16 = jnp.bfloat16

DEPTH = 2
DN_ALPHA = (2.0 * DEPTH) ** 0.25
LN_EPS = 1e-5
RMS_EPS = 1e-6

A_HEADS = 16
A_QK_DIM = 64
A_V_DIM = 64
A_LATENT = 256
IDX_HEADS = 8
IDX_DIM = 64
IDX_TOPK_MAX = 256
A_QBLK = 128
A_KBLK = 512
A_ONES = 16
CNT_ROWS = 32
BIS_STEPS = 4
LOG2E = 1.4426950408889634

B_HEADS = 4
B_GATE_RANK = 16
B_GATE_TAU = 16.0
B_CHUNK = 64
B_SUB = 16
B_TBLK = 512

N_EXPERTS = 32
TOP_K = 4
SWIGLU_LIMIT = 7.0
SWIGLU_ALPHA = 1.702
MOE_BLOCK = 256
ROW_TILE = 512
MOE_TOK_TILE = 256

VMEM_LIMIT = 56 * 1024 * 1024
NEG_BIG = -1e30


def _cparams(sem):
    return pltpu.CompilerParams(dimension_semantics=sem, vmem_limit_bytes=VMEM_LIMIT)


def _dot(a, b):
    return jnp.dot(a, b, preferred_element_type=F32)


def _dot_nt(a, b):
    return lax.dot_general(a, b, (((1,), (1,)), ((), ())), preferred_element_type=F32)


def _dot_tn(a, b):
    return lax.dot_general(a, b, (((0,), (0,)), ((), ())), preferred_element_type=F32)


def _layer_norm(z, g, b):
    mu = jnp.mean(z, axis=-1, keepdims=True)
    zc = z - mu
    var = jnp.mean(zc * zc, axis=-1, keepdims=True)
    return zc * lax.rsqrt(var + LN_EPS) * g + b


def _dsa_proj_kernel(x_ref, wqt_ref, wc_ref, wqit_ref, wki_ref, wwit_ref, kvn_ref, wukt_ref,
                     qlat_ref, c_ref, ct_ref, qi_ref, ki_ref, wi_ref):
    T = x_ref.shape[0]
    QB = A_QBLK
    xb = x_ref[...].astype(BF16)
    qt = _dot_nt(wqt_ref[...], xb)
    for h in range(A_HEADS):
        qh = qt[h * A_QK_DIM:(h + 1) * A_QK_DIM, :].astype(BF16)
        ql = (_dot(wukt_ref[h], qh) * (A_QK_DIM ** -0.5 * LOG2E)).astype(BF16)
        for j in range(T // QB):
            qlat_ref[j, :, h * QB:(h + 1) * QB] = ql[:, j * QB:(j + 1) * QB]
    c = _dot(xb, wc_ref[...])
    c = c * lax.rsqrt(jnp.mean(c * c, axis=-1, keepdims=True) + RMS_EPS) * kvn_ref[...]
    c_ref[...] = c.astype(BF16)
    ct_ref[:A_LATENT, :] = c.T.astype(BF16)
    ct_ref[A_LATENT:, :] = jnp.ones((A_ONES, T), BF16)
    qit = _dot_nt(wqit_ref[...], xb) * (IDX_DIM ** -0.5)
    for h in range(IDX_HEADS):
        qh = qit[h * IDX_DIM:(h + 1) * IDX_DIM, :].astype(BF16)
        for j in range(T // QB):
            qi_ref[j, :, h * QB:(h + 1) * QB] = qh[:, j * QB:(j + 1) * QB]
    ki_ref[...] = _dot(xb, wki_ref[...]).astype(BF16)
    wi_ref[...] = _dot_nt(wwit_ref[...], xb) * (IDX_HEADS ** -0.5)


def _dsa_proj(x, w_in, kv_norm, w_uk):
    Bsz, L, D = x.shape
    T, QB = A_KBLK, A_QBLK
    nkb, nqb, qpt = L // T, L // QB, T // QB
    o1 = A_HEADS * A_QK_DIM
    o2 = o1 + A_LATENT
    o3 = o2 + IDX_HEADS * IDX_DIM
    o4 = o3 + IDX_DIM
    wb = w_in.astype(BF16)
    wqt, wc, wqit, wki, wwit = wb[:, :o1].T, wb[:, o1:o2], wb[:, o2:o3].T, wb[:, o3:o4], wb[:, o4:].T
    wukt = jnp.swapaxes(w_uk, 1, 2).astype(BF16)
    full = lambda shape: pl.BlockSpec(shape, lambda b, t: (0,) * len(shape))
    return pl.pallas_call(
        _dsa_proj_kernel,
        grid=(Bsz, nkb),
        in_specs=[
            pl.BlockSpec((None, T, D), lambda b, t: (b, t, 0)),
            full(wqt.shape), full(wc.shape), full(wqit.shape), full(wki.shape), full(wwit.shape),
            full((1, A_LATENT)), full(wukt.shape),
        ],
        out_specs=[
            pl.BlockSpec((None, qpt, A_LATENT, A_HEADS * QB), lambda b, t: (b, t, 0, 0)),
            pl.BlockSpec((None, None, T, A_LATENT), lambda b, t: (b, t, 0, 0)),
            pl.BlockSpec((None, None, A_LATENT + A_ONES, T), lambda b, t: (b, t, 0, 0)),
            pl.BlockSpec((None, qpt, IDX_DIM, IDX_HEADS * QB), lambda b, t: (b, t, 0, 0)),
            pl.BlockSpec((None, None, T, IDX_DIM), lambda b, t: (b, t, 0, 0)),
            pl.BlockSpec((None, IDX_HEADS, T), lambda b, t: (b, 0, t)),
        ],
        out_shape=[
            jax.ShapeDtypeStruct((Bsz, nqb, A_LATENT, A_HEADS * QB), BF16),
            jax.ShapeDtypeStruct((Bsz, nkb, T, A_LATENT), BF16),
            jax.ShapeDtypeStruct((Bsz, nkb, A_LATENT + A_ONES, T), BF16),
            jax.ShapeDtypeStruct((Bsz, nqb, IDX_DIM, IDX_HEADS * QB), BF16),
            jax.ShapeDtypeStruct((Bsz, nkb, T, IDX_DIM), BF16),
            jax.ShapeDtypeStruct((Bsz, IDX_HEADS, L), F32),
        ],
        compiler_params=_cparams(("parallel", "parallel")),
        name="dsa_proj",
    )(x, wqt, wc, wqit, wki, wwit, kv_norm.reshape(1, A_LATENT), wukt)


def _ordered_bits_to_float(u):
    o = u ^ jnp.int32(-2 ** 31)
    bits = jnp.where(o >= 0, o, o ^ jnp.int32(2 ** 31 - 1))
    return lax.bitcast_convert_type(bits, F32)


def _dsa_attn_kernel(qi_ref, wi_ref, ki_ref, qlat_ref, ct_ref, c_ref, wuvt_ref, o_ref,
                     sc_ref, s_ref, m_ref, acc_ref, ot_ref, jcut_ref, *, topk, seq_len):
    QB, KB = A_QBLK, A_KBLK
    qb = pl.program_id(1)
    q0 = qb * QB
    nvb = (q0 + QB + KB - 1) // KB
    qpos = q0 + lax.broadcasted_iota(jnp.int32, (1, QB), 1)
    krow = lax.broadcasted_iota(jnp.int32, (KB, QB), 0)
    kf = float(topk)

    wi = wi_ref[...]
    qi = qi_ref[...]

    def score_body(kb, carry):
        s = _dot(ki_ref[kb], qi)
        sc = jnp.zeros((KB, QB), F32)
        for h in range(IDX_HEADS):
            sc = sc + jnp.maximum(s[:, h * QB:(h + 1) * QB], 0.0) * wi[h:h + 1, :]
        sc_ref[kb] = jnp.where((kb * KB + krow) <= qpos, sc, -jnp.inf)
        return carry

    lax.fori_loop(0, nvb, score_body, 0)

    def count(pred):
        def body(kb, acc):
            hit = jnp.where(pred(sc_ref[kb], kb), 1.0, 0.0)
            return acc + jnp.sum(hit.reshape(KB // CNT_ROWS, CNT_ROWS, QB), axis=0)
        acc = lax.fori_loop(0, nvb, body, jnp.zeros((CNT_ROWS, QB), F32))
        return jnp.sum(acc, axis=0, keepdims=True)

    def bis_cond(st):
        i, _, _, done = st
        return (i < 32) & (jnp.min(done) < 0.5)

    def bis_body(st):
        i, u, thr, done = st
        for step in range(BIS_STEPS):
            trial = u | jnp.left_shift(jnp.int32(1), 31 - (i + step))
            cand = _ordered_bits_to_float(trial)
            cnt = count(lambda blk, kb: blk >= cand)
            u = jnp.where(cnt >= kf, trial, u)
            hit = (cnt == kf) & (done < 0.5)
            thr = jnp.where(hit, cand, thr)
            done = jnp.where(hit, 1.0, done)
        return i + BIS_STEPS, u, thr, done

    done0 = jnp.where(qpos < topk, 1.0, 0.0)
    _, u, thr, done = lax.while_loop(
        bis_cond, bis_body,
        (jnp.int32(0), jnp.zeros((1, QB), jnp.int32), jnp.full((1, QB), -jnp.inf, F32), done0))
    thr = jnp.where(done > 0.5, thr, _ordered_bits_to_float(u))
    jcut_ref[...] = jnp.full((1, QB), seq_len, jnp.int32)

    @pl.when(jnp.min(done) < 0.5)
    def _():
        need = kf - count(lambda blk, kb: blk > thr)
        nbits = max(1, (seq_len - 1).bit_length())

        def jbody(i, x):
            trial = x | jnp.left_shift(jnp.int32(1), nbits - 1 - i)
            g = count(lambda blk, kb: (blk == thr) & ((kb * KB + krow) < trial))
            return jnp.where(g < need, trial, x)

        x = lax.fori_loop(0, nbits, jbody, jnp.zeros((1, QB), jnp.int32))
        jcut_ref[...] = jnp.where(done > 0.5, seq_len, x)

    jcut = jcut_ref[...]

    m_ref[...] = jnp.full(m_ref.shape, NEG_BIG, F32)
    acc_ref[...] = jnp.zeros(acc_ref.shape, F32)

    pair_cols = [slice(hp * 2 * QB, (hp + 1) * 2 * QB) for hp in range(A_HEADS // 2)]

    def logits(kb, cols):
        return _dot(c_ref[kb], qlat_ref[:, cols])

    def mask_bias(kb):
        blk = sc_ref[kb]
        kpos = kb * KB + krow
        sel = ((blk > thr) | ((blk == thr) & (kpos <= jcut))) & (kpos <= qpos)
        bias = jnp.where(sel, 0.0, NEG_BIG)
        return jnp.concatenate([bias, bias], axis=1)

    def consume(kb, s, cols, bias2):
        s = s + bias2
        m_prev = m_ref[:, cols]
        m_new = jnp.maximum(m_prev, jnp.max(s, axis=0, keepdims=True))
        alpha = jnp.exp2(m_prev - m_new)
        p = jnp.exp2(s - m_new).astype(BF16)
        acc_ref[:, cols] = alpha * acc_ref[:, cols] + _dot(ct_ref[kb], p)
        m_ref[:, cols] = m_new

    for cols in pair_cols:
        s_ref[:, cols] = logits(0, cols)

    def attn_body(kb, carry):
        bias2 = mask_bias(kb)
        for cols in pair_cols:
            s = s_ref[:, cols]
            s_ref[:, cols] = logits(kb + 1, cols)
            consume(kb, s, cols, bias2)
        return carry

    lax.fori_loop(0, nvb - 1, attn_body, 0)
    bias2 = mask_bias(nvb - 1)
    for cols in pair_cols:
        consume(nvb - 1, s_ref[:, cols], cols, bias2)

    for h in range(A_HEADS):
        cols = slice(h * QB, (h + 1) * QB)
        oh = acc_ref[:A_LATENT, cols] / acc_ref[A_LATENT:A_LATENT + 1, cols]
        ot_ref[h * A_V_DIM:(h + 1) * A_V_DIM, :] = _dot(wuvt_ref[h], oh.astype(BF16))
    o_ref[...] = ot_ref[...].T.astype(BF16)


def _dsa_attention(qi, wi, ki, qlat, ct, c, w_uv):
    Bsz, nqb, _, _ = qlat.shape
    QB, KB = A_QBLK, A_KBLK
    L = nqb * QB
    nkb = L // KB
    topk = min(IDX_TOPK_MAX, L // 4)
    wuvt = jnp.swapaxes(w_uv, 1, 2).astype(BF16)
    kern = functools.partial(_dsa_attn_kernel, topk=topk, seq_len=L)
    return pl.pallas_call(
        kern,
        grid=(Bsz, nqb),
        in_specs=[
            pl.BlockSpec((None, None, IDX_DIM, IDX_HEADS * QB), lambda b, q: (b, q, 0, 0)),
            pl.BlockSpec((None, IDX_HEADS, QB), lambda b, q: (b, 0, q)),
            pl.BlockSpec((None, nkb, KB, IDX_DIM), lambda b, q: (b, 0, 0, 0)),
            pl.BlockSpec((None, None, A_LATENT, A_HEADS * QB), lambda b, q: (b, q, 0, 0)),
            pl.BlockSpec((None, nkb, A_LATENT + A_ONES, KB), lambda b, q: (b, 0, 0, 0)),
            pl.BlockSpec((None, nkb, KB, A_LATENT), lambda b, q: (b, 0, 0, 0)),
            pl.BlockSpec((A_HEADS, A_V_DIM, A_LATENT), lambda b, q: (0, 0, 0)),
        ],
        out_specs=pl.BlockSpec((None, QB, A_HEADS * A_V_DIM), lambda b, q: (b, q, 0)),
        out_shape=jax.ShapeDtypeStruct((Bsz, L, A_HEADS * A_V_DIM), BF16),
        scratch_shapes=[
            pltpu.VMEM((nkb, KB, QB), F32),
            pltpu.VMEM((KB, A_HEADS * QB), F32),
            pltpu.VMEM((1, A_HEADS * QB), F32),
            pltpu.VMEM((A_LATENT + A_ONES, A_HEADS * QB), F32),
            pltpu.VMEM((A_HEADS * A_V_DIM, QB), F32),
            pltpu.VMEM((1, QB), jnp.int32),
        ],
        compiler_params=_cparams(("parallel", "parallel")),
        name="dsa_attn",
    )(qi, wi, ki, qlat, ct, c, wuvt)


def _outproj_ln_kernel(o_ref, w_ref, x_ref, g_ref, b_ref, y_ref):
    h = _dot(o_ref[...], w_ref[...])
    y_ref[...] = _layer_norm(DN_ALPHA * x_ref[...] + h, g_ref[...], b_ref[...])


def _outproj_ln(o, w_out, x, g, b):
    N, K = o.shape
    D = x.shape[1]
    T = ROW_TILE
    return pl.pallas_call(
        _outproj_ln_kernel,
        grid=(N // T,),
        in_specs=[
            pl.BlockSpec((T, K), lambda i: (i, 0)),
            pl.BlockSpec((K, D), lambda i: (0, 0)),
            pl.BlockSpec((T, D), lambda i: (i, 0)),
            pl.BlockSpec((1, D), lambda i: (0, 0)),
            pl.BlockSpec((1, D), lambda i: (0, 0)),
        ],
        out_specs=pl.BlockSpec((T, D), lambda i: (i, 0)),
        out_shape=jax.ShapeDtypeStruct((N, D), F32),
        compiler_params=_cparams(("parallel",)),
        name="outproj_ln",
    )(o, w_out.astype(BF16), x, g.reshape(1, D), b.reshape(1, D))


def _gla_proj_kernel(x_ref, wq_ref, wk_ref, wv_ref, wg_ref, wr_ref, wg2_ref, gb_ref,
                     q_ref, k_ref, v_ref, la_ref, r_ref, *, kh):
    xb = x_ref[...].astype(BF16)
    q_ref[...] = _dot(xb, wq_ref[...]) * (kh ** -0.5)
    k_ref[...] = _dot(xb, wk_ref[...])
    v_ref[...] = _dot(xb, wv_ref[...]).astype(BF16)
    r_ref[...] = _dot(xb, wr_ref[...])
    glr = _dot(xb, wg_ref[...])
    z = _dot(glr.astype(BF16), wg2_ref[...]) + gb_ref[...]
    log_sig = jnp.minimum(z, 0.0) - jnp.log(1.0 + jnp.exp(-jnp.abs(z)))
    la_ref[...] = log_sig / B_GATE_TAU


def _gla_proj(x, w_in, w_g2, g_bias):
    N, D = x.shape
    DK = w_g2.shape[1]
    DV = D
    T = ROW_TILE
    wb = w_in.astype(BF16)
    o1, o2, o3, o4 = DK, 2 * DK, 2 * DK + DV, 2 * DK + DV + B_GATE_RANK
    wq, wk, wv, wg, wr = wb[:, :o1], wb[:, o1:o2], wb[:, o2:o3], wb[:, o3:o4], wb[:, o4:]
    full = lambda shape: pl.BlockSpec(shape, lambda i: (0,) * len(shape))
    row = lambda w: pl.BlockSpec((T, w), lambda i: (i, 0))
    kern = functools.partial(_gla_proj_kernel, kh=DK // B_HEADS)
    return pl.pallas_call(
        kern,
        grid=(N // T,),
        in_specs=[row(D), full(wq.shape), full(wk.shape), full(wv.shape), full(wg.shape),
                  full(wr.shape), full(w_g2.shape), full((1, DK))],
        out_specs=[row(DK), row(DK), row(DV), row(DK), row(DV)],
        out_shape=[
            jax.ShapeDtypeStruct((N, DK), F32),
            jax.ShapeDtypeStruct((N, DK), F32),
            jax.ShapeDtypeStruct((N, DV), BF16),
            jax.ShapeDtypeStruct((N, DK), F32),
            jax.ShapeDtypeStruct((N, DV), F32),
        ],
        compiler_params=_cparams(("parallel",)),
        name="gla_proj",
    )(x, wq, wk, wv, wg, wr, w_g2.astype(BF16), g_bias.reshape(1, DK))


def _split3(a):
    hi = a.astype(BF16)
    r1 = a - hi.astype(F32)
    mid = r1.astype(BF16)
    lo = (r1 - mid.astype(F32)).astype(BF16)
    return hi, mid, lo


def _gla_kernel(q_ref, k_ref, v_ref, la_ref, r_ref, ng_ref, o_ref, st_ref, a_ref):
    C, SB = B_CHUNK, B_SUB
    nsub = C // SB

    @pl.when(pl.program_id(2) == 0)
    def _():
        st_ref[...] = jnp.zeros(st_ref.shape, F32)

    ri = lax.broadcasted_iota(jnp.int32, (C, C), 0)
    ci = lax.broadcasted_iota(jnp.int32, (C, C), 1)
    tril = jnp.where(ci <= ri, 1.0, 0.0).astype(BF16)
    sub_r = lax.broadcasted_iota(jnp.int32, (SB, 1), 0)
    sub_c = lax.broadcasted_iota(jnp.int32, (1, SB), 1)

    def chunk_body(ch, carry):
        rows = pl.ds(pl.multiple_of(ch * C, C), C)
        q = q_ref[rows, :]
        k = k_ref[rows, :]
        v = v_ref[rows, :]
        la = la_ref[rows, :]
        hi, mid, lo = _split3(la)
        b = _dot(tril, hi) + _dot(tril, mid) + _dot(tril, lo)
        st = st_ref[...]
        o = _dot_nt((q * jnp.exp(b)).astype(BF16), st.astype(BF16))

        a_ref[...] = jnp.zeros(a_ref.shape, F32)
        for I in range(nsub):
            r0 = I * SB
            bI = b[r0:r0 + SB, :]
            qI = q[r0:r0 + SB, :]
            kI = k[r0:r0 + SB, :]
            if I > 0:
                ref_lvl = b[r0 - 1:r0, :]
                qs = (qI * jnp.exp(bI - ref_lvl)).astype(BF16)
                ks = (k[:r0, :] * jnp.exp(ref_lvl - b[:r0, :])).astype(BF16)
                a_ref[r0:r0 + SB, :r0] = _dot_nt(qs, ks)
            diag = jnp.zeros((SB, SB), F32)
            for j in range(SB):
                dlt = jnp.where(sub_r >= j, bI - bI[j:j + 1, :], -jnp.inf)
                col = jnp.sum(qI * kI[j:j + 1, :] * jnp.exp(dlt), axis=1, keepdims=True)
                diag = diag + col * jnp.where(sub_c == j, 1.0, 0.0)
            a_ref[r0:r0 + SB, r0:r0 + SB] = diag
        o = o + _dot(a_ref[...].astype(BF16), v)

        b_last = b[C - 1:C, :]
        kd = (k * jnp.exp(b_last - b)).astype(BF16)
        st_ref[...] = st * jnp.exp(b_last) + _dot_tn(v, kd)

        o = o * lax.rsqrt(jnp.mean(o * o, axis=-1, keepdims=True) + RMS_EPS) * ng_ref[...]
        r = r_ref[rows, :]
        o_ref[rows, :] = (o * (r / (1.0 + jnp.exp(-r)))).astype(BF16)
        return carry

    lax.fori_loop(0, q_ref.shape[0] // C, chunk_body, 0)


def _gla_core(q, k, v, la, r, norm_g):
    Bsz, L, DK = q.shape
    DV = v.shape[2]
    kh, vh = DK // B_HEADS, DV // B_HEADS
    T = B_TBLK
    kspec = pl.BlockSpec((None, T, kh), lambda b, h, t: (b, t, h))
    vspec = pl.BlockSpec((None, T, vh), lambda b, h, t: (b, t, h))
    return pl.pallas_call(
        _gla_kernel,
        grid=(Bsz, B_HEADS, L // T),
        in_specs=[kspec, kspec, vspec, kspec, vspec,
                  pl.BlockSpec((1, vh), lambda b, h, t: (0, h))],
        out_specs=vspec,
        out_shape=jax.ShapeDtypeStruct((Bsz, L, DV), BF16),
        scratch_shapes=[pltpu.VMEM((vh, kh), F32), pltpu.VMEM((B_CHUNK, B_CHUNK), F32)],
        compiler_params=_cparams(("parallel", "parallel", "arbitrary")),
        name="gla_core",
    )(q, k, v, la, r, norm_g.reshape(1, DV))


def _router_kernel(x_ref, w_ref, b_ref, idx_ref, gate_ref, rank_ref, cnt_ref, carry_ref):
    T = x_ref.shape[0]
    E = N_EXPERTS

    @pl.when(pl.program_id(0) == 0)
    def _():
        carry_ref[...] = jnp.zeros(carry_ref.shape, F32)

    x = x_ref[...]
    w = w_ref[...]
    xh = x.astype(BF16)
    xl = (x - xh.astype(F32)).astype(BF16)
    wh = w.astype(BF16)
    wl = (w - wh.astype(F32)).astype(BF16)
    logits = _dot(xh, wh) + (_dot(xh, wl) + _dot(xl, wh)) + b_ref[...]

    eidx = lax.broadcasted_iota(jnp.int32, (T, E), 1)
    work = logits
    vals, idxs = [], []
    for _ in range(TOP_K):
        mx = jnp.max(work, axis=1, keepdims=True)
        ix = jnp.min(jnp.where(work == mx, eidx, E), axis=1, keepdims=True)
        vals.append(mx)
        idxs.append(ix)
        work = jnp.where(eidx == ix, -jnp.inf, work)
    exps = [jnp.exp(vv - vals[0]) for vv in vals]
    den = exps[0] + exps[1] + exps[2] + exps[3]

    onehot = jnp.zeros((T, E), F32)
    for ix in idxs:
        onehot = onehot + jnp.where(eidx == ix, 1.0, 0.0)
    ri = lax.broadcasted_iota(jnp.int32, (T, T), 0)
    ci = lax.broadcasted_iota(jnp.int32, (T, T), 1)
    strict = jnp.where(ci < ri, 1.0, 0.0).astype(BF16)
    prefix = _dot(strict, onehot.astype(BF16)) + carry_ref[...]
    carry_ref[...] = carry_ref[...] + jnp.sum(onehot, axis=0, keepdims=True)
    cnt_ref[...] = carry_ref[...]

    for j in range(TOP_K):
        idx_ref[:, j:j + 1] = idxs[j]
        gate_ref[:, j:j + 1] = exps[j] / den
        rk = jnp.sum(jnp.where(eidx == idxs[j], prefix, 0.0), axis=1, keepdims=True)
        rank_ref[:, j:j + 1] = rk.astype(jnp.int32)


def _router(xt, w_router, b_router):
    N, D = xt.shape
    T = MOE_TOK_TILE
    tok = pl.BlockSpec((T, TOP_K), lambda i: (i, 0))
    return pl.pallas_call(
        _router_kernel,
        grid=(N // T,),
        in_specs=[
            pl.BlockSpec((T, D), lambda i: (i, 0)),
            pl.BlockSpec((D, N_EXPERTS), lambda i: (0, 0)),
            pl.BlockSpec((1, N_EXPERTS), lambda i: (0, 0)),
        ],
        out_specs=[tok, tok, tok, pl.BlockSpec((1, N_EXPERTS), lambda i: (0, 0))],
        out_shape=[
            jax.ShapeDtypeStruct((N, TOP_K), jnp.int32),
            jax.ShapeDtypeStruct((N, TOP_K), F32),
            jax.ShapeDtypeStruct((N, TOP_K), jnp.int32),
            jax.ShapeDtypeStruct((1, N_EXPERTS), F32),
        ],
        scratch_shapes=[pltpu.VMEM((1, N_EXPERTS), F32)],
        compiler_params=_cparams(("arbitrary",)),
        name="moe_router",
    )(xt, w_router, b_router.reshape(1, N_EXPERTS))


def _dispatch_kernel(dest_ref, x_ref, xs_in_ref, xs_ref, sem):
    del xs_in_ref
    T = x_ref.shape[0]

    def row_copies(r):
        src = x_ref.at[pl.ds(r, 1)]
        return [pltpu.make_async_copy(src, xs_ref.at[pl.ds(dest_ref[r * TOP_K + j], 1)], sem)
                for j in range(TOP_K)]

    def issue(r, c):
        for cp in row_copies(r):
            cp.start()
        return c

    def drain(r, c):
        for cp in row_copies(r):
            cp.wait()
        return c

    lax.fori_loop(0, T, issue, 0, unroll=2)
    lax.fori_loop(0, T, drain, 0, unroll=2)


def _dispatch(xt, dest_flat, n_rows):
    N, D = xt.shape
    T = MOE_TOK_TILE
    xs0 = jnp.zeros((n_rows, D), F32)
    return pl.pallas_call(
        _dispatch_kernel,
        grid=(N // T,),
        in_specs=[
            pl.BlockSpec((T * TOP_K,), lambda i: (i,), memory_space=pltpu.SMEM),
            pl.BlockSpec((T, D), lambda i: (i, 0)),
            pl.BlockSpec(memory_space=pl.ANY),
        ],
        out_specs=pl.BlockSpec(memory_space=pl.ANY),
        out_shape=jax.ShapeDtypeStruct((n_rows, D), F32),
        scratch_shapes=[pltpu.SemaphoreType.DMA],
        input_output_aliases={2: 0},
        compiler_params=_cparams(("arbitrary",)),
        name="moe_dispatch",
    )(dest_flat, xt, xs0)


def _expert_kernel(blk_e_ref, nused_ref, xs_ref, w1_ref, b1_ref, w2_ref, b2_ref, ys_ref,
                   w1b_ref, w2b_ref):
    i = pl.program_id(0)
    prev = blk_e_ref[jnp.maximum(i - 1, 0)]
    fresh = (i == 0) | (blk_e_ref[i] != prev)

    @pl.when(fresh)
    def _():
        w1b_ref[...] = w1_ref[...].astype(BF16)
        w2b_ref[...] = w2_ref[...].astype(BF16)

    @pl.when(i < nused_ref[0])
    def _():
        F = w2_ref.shape[0]
        h = _dot(xs_ref[...].astype(BF16), w1b_ref[...]) + b1_ref[...]
        g = jnp.minimum(h[:, :F], SWIGLU_LIMIT)
        u = jnp.clip(h[:, F:], -SWIGLU_LIMIT, SWIGLU_LIMIT)
        glu = g / (1.0 + jnp.exp(-SWIGLU_ALPHA * g))
        act = ((u + 1.0) * glu).astype(BF16)
        ys_ref[...] = _dot(act, w2b_ref[...]) + b2_ref[...]

    @pl.when(i >= nused_ref[0])
    def _():
        ys_ref[...] = jnp.zeros(ys_ref.shape, F32)


def _experts(xs, blk_e, n_used, w1, b1, w2, b2, layer):
    P, D = xs.shape
    _, E, _, F2 = w1.shape
    F = w2.shape[2]
    nb = P // MOE_BLOCK
    wspec = lambda r, c: pl.BlockSpec((None, None, r, c), lambda i, be, nu: (layer, be[i], 0, 0))
    grid_spec = pltpu.PrefetchScalarGridSpec(
        num_scalar_prefetch=2,
        grid=(nb,),
        in_specs=[
            pl.BlockSpec((MOE_BLOCK, D), lambda i, be, nu: (i, 0)),
            wspec(D, F2), wspec(1, F2), wspec(F, D), wspec(1, D),
        ],
        out_specs=pl.BlockSpec((MOE_BLOCK, D), lambda i, be, nu: (i, 0)),
        scratch_shapes=[pltpu.VMEM((D, F2), BF16), pltpu.VMEM((F, D), BF16)],
    )
    nl = w1.shape[0]
    return pl.pallas_call(
        _expert_kernel,
        grid_spec=grid_spec,
        out_shape=jax.ShapeDtypeStruct((P, D), F32),
        compiler_params=_cparams(("arbitrary",)),
        name="moe_experts",
    )(blk_e, n_used, xs, w1, b1.reshape(nl, E, 1, F2), w2, b2.reshape(nl, E, 1, D))


def _combine_ln_kernel(dest_ref, gate_ref, x_ref, ys_ref, g_ref, b_ref, y_ref, buf_ref, sem):
    T = x_ref.shape[0]

    def row_copies(r):
        return [pltpu.make_async_copy(ys_ref.at[pl.ds(dest_ref[r * TOP_K + j], 1)],
                                      buf_ref.at[j, pl.ds(r, 1)], sem)
                for j in range(TOP_K)]

    def issue(r, c):
        for cp in row_copies(r):
            cp.start()
        return c

    def drain(r, c):
        for cp in row_copies(r):
            cp.wait()
        return c

    lax.fori_loop(0, T, issue, 0, unroll=2)
    lax.fori_loop(0, T, drain, 0, unroll=2)

    gate = gate_ref[...]
    f = jnp.zeros(x_ref.shape, F32)
    for j in range(TOP_K):
        f = f + buf_ref[j] * gate[:, j:j + 1]
    y_ref[...] = _layer_norm(DN_ALPHA * x_ref[...] + f, g_ref[...], b_ref[...])


def _combine_ln(xt, ys, dest_flat, gates, g, b):
    N, D = xt.shape
    T = MOE_TOK_TILE
    return pl.pallas_call(
        _combine_ln_kernel,
        grid=(N // T,),
        in_specs=[
            pl.BlockSpec((T * TOP_K,), lambda i: (i,), memory_space=pltpu.SMEM),
            pl.BlockSpec((T, TOP_K), lambda i: (i, 0)),
            pl.BlockSpec((T, D), lambda i: (i, 0)),
            pl.BlockSpec(memory_space=pl.ANY),
            pl.BlockSpec((1, D), lambda i: (0, 0)),
            pl.BlockSpec((1, D), lambda i: (0, 0)),
        ],
        out_specs=pl.BlockSpec((T, D), lambda i: (i, 0)),
        out_shape=jax.ShapeDtypeStruct((N, D), F32),
        scratch_shapes=[pltpu.VMEM((TOP_K, T, D), F32), pltpu.SemaphoreType.DMA],
        compiler_params=_cparams(("arbitrary",)),
        name="moe_combine_ln",
    )(dest_flat, gates, xt, ys, g.reshape(1, D), b.reshape(1, D))


def _moe_ln(xt, w_router, b_router, w1, b1, w2, b2, g, b, layer):
    N, D = xt.shape
    idx, gates, rank, counts = _router(xt, w_router, b_router)
    counts = counts.reshape(N_EXPERTS).astype(jnp.int32)
    padded = ((counts + MOE_BLOCK - 1) // MOE_BLOCK) * MOE_BLOCK
    pend = jnp.cumsum(padded)
    pstart = pend - padded
    n_blocks = -(-(N * TOP_K) // MOE_BLOCK) + N_EXPERTS
    dest = (pstart[idx] + rank).reshape(N * TOP_K)
    blk_start = jnp.arange(n_blocks, dtype=jnp.int32) * MOE_BLOCK
    blk_e = jnp.minimum(jnp.sum((pend[None, :] <= blk_start[:, None]).astype(jnp.int32), axis=1),
                        N_EXPERTS - 1)
    n_used = (pend[-1:] // MOE_BLOCK).astype(jnp.int32)
    xs = _dispatch(xt, dest, n_blocks * MOE_BLOCK)
    ys = _experts(xs, blk_e, n_used, w1, b1, w2, b2, layer)
    return _combine_ln(xt, ys, dest, gates, g, b)


def _dsa_layer(x, w_in, kv_norm, w_uk, w_uv, w_out, g, b):
    Bsz, L, D = x.shape
    qlat, c, ct, qi, ki, wi = _dsa_proj(x, w_in, kv_norm, w_uk)
    o = _dsa_attention(qi, wi, ki, qlat, ct, c, w_uv)
    return _outproj_ln(o.reshape(Bsz * L, -1), w_out, x.reshape(Bsz * L, D), g, b)


def _gla_layer(x, w_in, w_g2, g_bias, norm_g, w_out, g, b):
    Bsz, L, D = x.shape
    xt = x.reshape(Bsz * L, D)
    q, k, v, la, r = _gla_proj(xt, w_in, w_g2, g_bias)
    sh = lambda t: t.reshape(Bsz, L, t.shape[-1])
    o = _gla_core(sh(q), sh(k), sh(v), sh(la), sh(r), norm_g)
    return _outproj_ln(o.reshape(Bsz * L, -1), w_out, xt, g, b)


def kernel(x, a_w_in, a_kv_norm, a_w_uk, a_w_uv, a_w_out, b_w_in, b_w_g2, b_g_bias, b_norm, b_w_out, m_w_router, m_b_router, m_w1, m_b1, m_w2, m_b2, ln1_g, ln1_b, ln2_g, ln2_b):
    Bsz, L, D = x.shape
    for i in range(DEPTH):
        j = i // 2
        if i % 2 == 0:
            xt = _dsa_layer(x, a_w_in[j], a_kv_norm[j], a_w_uk[j], a_w_uv[j], a_w_out[j],
                            ln1_g[i], ln1_b[i])
        else:
            xt = _gla_layer(x, b_w_in[j], b_w_g2[j], b_g_bias[j], b_norm[j], b_w_out[j],
                            ln1_g[i], ln1_b[i])
        xt = _moe_ln(xt, m_w_router[i], m_b_router[i], m_w1, m_b1, m_w2, m_b2,
                     ln2_g[i], ln2_b[i], i)
        x = xt.reshape(Bsz, L, D)
    return x
```

```python
import functools

import jax
import jax.numpy as jnp
from jax import lax
from jax.experimental import pallas as pl
from jax.experimental.pallas import tpu as pltpu

F32 = jnp.float32
BF16 = jnp.bfloat16

DEPTH = 2
DN_ALPHA = (2.0 * DEPTH) ** 0.25
LN_EPS = 1e-5
RMS_EPS = 1e-6

A_HEADS = 16
A_QK_DIM = 64
A_V_DIM = 64
A_LATENT = 256
IDX_HEADS = 8
IDX_DIM = 64
IDX_TOPK_MAX = 256
A_QBLK = 128
A_KBLK = 512
A_ONES = 16
CNT_ROWS = 32
BIS_STEPS = 4
LOG2E = 1.4426950408889634

B_HEADS = 4
B_GATE_RANK = 16
B_GATE_TAU = 16.0
B_CHUNK = 64
B_SUB = 16
B_TBLK = 512

N_EXPERTS = 32
TOP_K = 4
SWIGLU_LIMIT = 7.0
SWIGLU_ALPHA = 1.702
MOE_BLOCK = 256
MOE_CHUNK = 256
ROW_TILE = 512
MOE_TOK_TILE = 256

VMEM_LIMIT = 56 * 1024 * 1024
NEG_BIG = -1e30


def _cparams(sem):
    return pltpu.CompilerParams(dimension_semantics=sem, vmem_limit_bytes=VMEM_LIMIT)


def _dot(a, b):
    return jnp.dot(a, b, preferred_element_type=F32)


def _dot_nt(a, b):
    return lax.dot_general(a, b, (((1,), (1,)), ((), ())), preferred_element_type=F32)


def _dot_tn(a, b):
    return lax.dot_general(a, b, (((0,), (0,)), ((), ())), preferred_element_type=F32)


def _layer_norm(z, g, b):
    mu = jnp.mean(z, axis=-1, keepdims=True)
    zc = z - mu
    var = jnp.mean(zc * zc, axis=-1, keepdims=True)
    return zc * lax.rsqrt(var + LN_EPS) * g + b


def _dsa_proj_kernel(x_ref, wqt_ref, wc_ref, wqit_ref, wki_ref, wwit_ref, kvn_ref, wukt_ref,
                     qlat_ref, c_ref, ct_ref, qi_ref, ki_ref, wi_ref):
    T = x_ref.shape[0]
    QB = A_QBLK
    xb = x_ref[...].astype(BF16)
    qt = _dot_nt(wqt_ref[...], xb)
    for h in range(A_HEADS):
        qh = qt[h * A_QK_DIM:(h + 1) * A_QK_DIM, :].astype(BF16)
        ql = (_dot(wukt_ref[h], qh) * (A_QK_DIM ** -0.5 * LOG2E)).astype(BF16)
        for j in range(T // QB):
            qlat_ref[j, :, h * QB:(h + 1) * QB] = ql[:, j * QB:(j + 1) * QB]
    c = _dot(xb, wc_ref[...])
    c = c * lax.rsqrt(jnp.mean(c * c, axis=-1, keepdims=True) + RMS_EPS) * kvn_ref[...]
    c_ref[...] = c.astype(BF16)
    ct_ref[:A_LATENT, :] = c.T.astype(BF16)
    ct_ref[A_LATENT:, :] = jnp.ones((A_ONES, T), BF16)
    qit = _dot_nt(wqit_ref[...], xb) * (IDX_DIM ** -0.5)
    for h in range(IDX_HEADS):
        qh = qit[h * IDX_DIM:(h + 1) * IDX_DIM, :].astype(BF16)
        for j in range(T // QB):
            qi_ref[j, :, h * QB:(h + 1) * QB] = qh[:, j * QB:(j + 1) * QB]
    ki_ref[...] = _dot(xb, wki_ref[...]).astype(BF16)
    wi_ref[...] = _dot_nt(wwit_ref[...], xb) * (IDX_HEADS ** -0.5)


def _dsa_proj(x, w_in, kv_norm, w_uk):
    Bsz, L, D = x.shape
    T, QB = A_KBLK, A_QBLK
    nkb, nqb, qpt = L // T, L // QB, T // QB
    o1 = A_HEADS * A_QK_DIM
    o2 = o1 + A_LATENT
    o3 = o2 + IDX_HEADS * IDX_DIM
    o4 = o3 + IDX_DIM
    wb = w_in.astype(BF16)
    wqt, wc, wqit, wki, wwit = wb[:, :o1].T, wb[:, o1:o2], wb[:, o2:o3].T, wb[:, o3:o4], wb[:, o4:].T
    wukt = jnp.swapaxes(w_uk, 1, 2).astype(BF16)
    full = lambda shape: pl.BlockSpec(shape, lambda b, t: (0,) * len(shape))
    return pl.pallas_call(
        _dsa_proj_kernel,
        grid=(Bsz, nkb),
        in_specs=[
            pl.BlockSpec((None, T, D), lambda b, t: (b, t, 0)),
            full(wqt.shape), full(wc.shape), full(wqit.shape), full(wki.shape), full(wwit.shape),
            full((1, A_LATENT)), full(wukt.shape),
        ],
        out_specs=[
            pl.BlockSpec((None, qpt, A_LATENT, A_HEADS * QB), lambda b, t: (b, t, 0, 0)),
            pl.BlockSpec((None, None, T, A_LATENT), lambda b, t: (b, t, 0, 0)),
            pl.BlockSpec((None, None, A_LATENT + A_ONES, T), lambda b, t: (b, t, 0, 0)),
            pl.BlockSpec((None, qpt, IDX_DIM, IDX_HEADS * QB), lambda b, t: (b, t, 0, 0)),
            pl.BlockSpec((None, None, T, IDX_DIM), lambda b, t: (b, t, 0, 0)),
            pl.BlockSpec((None, IDX_HEADS, T), lambda b, t: (b, 0, t)),
        ],
        out_shape=[
            jax.ShapeDtypeStruct((Bsz, nqb, A_LATENT, A_HEADS * QB), BF16),
            jax.ShapeDtypeStruct((Bsz, nkb, T, A_LATENT), BF16),
            jax.ShapeDtypeStruct((Bsz, nkb, A_LATENT + A_ONES, T), BF16),
            jax.ShapeDtypeStruct((Bsz, nqb, IDX_DIM, IDX_HEADS * QB), BF16),
            jax.ShapeDtypeStruct((Bsz, nkb, T, IDX_DIM), BF16),
            jax.ShapeDtypeStruct((Bsz, IDX_HEADS, L), F32),
        ],
        compiler_params=_cparams(("parallel", "parallel")),
        name="dsa_proj",
    )(x, wqt, wc, wqit, wki, wwit, kv_norm.reshape(1, A_LATENT), wukt)


def _ordered_bits_to_float(u):
    o = u ^ jnp.int32(-2 ** 31)
    bits = jnp.where(o >= 0, o, o ^ jnp.int32(2 ** 31 - 1))
    return lax.bitcast_convert_type(bits, F32)


def _dsa_attn_kernel(qi_ref, wi_ref, ki_ref, qlat_ref, ct_ref, c_ref, wuvt_ref, o_ref,
                     sc_ref, s_ref, m_ref, acc_ref, ot_ref, jcut_ref, *, topk, seq_len):
    QB, KB = A_QBLK, A_KBLK
    qb = pl.program_id(1)
    q0 = qb * QB
    nvb = (q0 + QB + KB - 1) // KB
    qpos = q0 + lax.broadcasted_iota(jnp.int32, (1, QB), 1)
    krow = lax.broadcasted_iota(jnp.int32, (KB, QB), 0)
    kf = float(topk)

    wi = wi_ref[...]
    qi = qi_ref[...]

    def score_body(kb, carry):
        s = _dot(ki_ref[kb], qi)
        sc = jnp.zeros((KB, QB), F32)
        for h in range(IDX_HEADS):
            sc = sc + jnp.maximum(s[:, h * QB:(h + 1) * QB], 0.0) * wi[h:h + 1, :]
        sc_ref[kb] = jnp.where((kb * KB + krow) <= qpos, sc, -jnp.inf)
        return carry

    lax.fori_loop(0, nvb, score_body, 0)

    def count(pred):
        def body(kb, acc):
            hit = jnp.where(pred(sc_ref[kb], kb), 1.0, 0.0)
            return acc + jnp.sum(hit.reshape(KB // CNT_ROWS, CNT_ROWS, QB), axis=0)
        acc = lax.fori_loop(0, nvb, body, jnp.zeros((CNT_ROWS, QB), F32))
        return jnp.sum(acc, axis=0, keepdims=True)

    def bis_cond(st):
        i, _, _, done = st
        return (i < 32) & (jnp.min(done) < 0.5)

    def bis_body(st):
        i, u, thr, done = st
        for step in range(BIS_STEPS):
            trial = u | jnp.left_shift(jnp.int32(1), 31 - (i + step))
            cand = _ordered_bits_to_float(trial)
            cnt = count(lambda blk, kb: blk >= cand)
            u = jnp.where(cnt >= kf, trial, u)
            hit = (cnt == kf) & (done < 0.5)
            thr = jnp.where(hit, cand, thr)
            done = jnp.where(hit, 1.0, done)
        return i + BIS_STEPS, u, thr, done

    done0 = jnp.where(qpos < topk, 1.0, 0.0)
    _, u, thr, done = lax.while_loop(
        bis_cond, bis_body,
        (jnp.int32(0), jnp.zeros((1, QB), jnp.int32), jnp.full((1, QB), -jnp.inf, F32), done0))
    thr = jnp.where(done > 0.5, thr, _ordered_bits_to_float(u))
    jcut_ref[...] = jnp.full((1, QB), seq_len, jnp.int32)

    @pl.when(jnp.min(done) < 0.5)
    def _():
        need = kf - count(lambda blk, kb: blk > thr)
        nbits = max(1, (seq_len - 1).bit_length())

        def jbody(i, x):
            trial = x | jnp.left_shift(jnp.int32(1), nbits - 1 - i)
            g = count(lambda blk, kb: (blk == thr) & ((kb * KB + krow) < trial))
            return jnp.where(g < need, trial, x)

        x = lax.fori_loop(0, nbits, jbody, jnp.zeros((1, QB), jnp.int32))
        jcut_ref[...] = jnp.where(done > 0.5, seq_len, x)

    jcut = jcut_ref[...]

    m_ref[...] = jnp.full(m_ref.shape, NEG_BIG, F32)
    acc_ref[...] = jnp.zeros(acc_ref.shape, F32)

    pair_cols = [slice(hp * 2 * QB, (hp + 1) * 2 * QB) for hp in range(A_HEADS // 2)]

    def logits(kb, cols):
        return _dot(c_ref[kb], qlat_ref[:, cols])

    def mask_bias(kb):
        blk = sc_ref[kb]
        kpos = kb * KB + krow
        sel = ((blk > thr) | ((blk == thr) & (kpos <= jcut))) & (kpos <= qpos)
        bias = jnp.where(sel, 0.0, NEG_BIG)
        return jnp.concatenate([bias, bias], axis=1)

    def consume(kb, s, cols, bias2):
        s = s + bias2
        m_prev = m_ref[:, cols]
        m_new = jnp.maximum(m_prev, jnp.max(s, axis=0, keepdims=True))
        alpha = jnp.exp2(m_prev - m_new)
        p = jnp.exp2(s - m_new).astype(BF16)
        acc_ref[:, cols] = alpha * acc_ref[:, cols] + _dot(ct_ref[kb], p)
        m_ref[:, cols] = m_new

    for cols in pair_cols:
        s_ref[:, cols] = logits(0, cols)

    def attn_body(kb, carry):
        bias2 = mask_bias(kb)
        for cols in pair_cols:
            s = s_ref[:, cols]
            s_ref[:, cols] = logits(kb + 1, cols)
            consume(kb, s, cols, bias2)
        return carry

    lax.fori_loop(0, nvb - 1, attn_body, 0)
    bias2 = mask_bias(nvb - 1)
    for cols in pair_cols:
        consume(nvb - 1, s_ref[:, cols], cols, bias2)

    for h in range(A_HEADS):
        cols = slice(h * QB, (h + 1) * QB)
        oh = acc_ref[:A_LATENT, cols] / acc_ref[A_LATENT:A_LATENT + 1, cols]
        ot_ref[h * A_V_DIM:(h + 1) * A_V_DIM, :] = _dot(wuvt_ref[h], oh.astype(BF16))
    o_ref[...] = ot_ref[...].T.astype(BF16)


def _dsa_attention(qi, wi, ki, qlat, ct, c, w_uv):
    Bsz, nqb, _, _ = qlat.shape
    QB, KB = A_QBLK, A_KBLK
    L = nqb * QB
    nkb = L // KB
    topk = min(IDX_TOPK_MAX, L // 4)
    wuvt = jnp.swapaxes(w_uv, 1, 2).astype(BF16)
    kern = functools.partial(_dsa_attn_kernel, topk=topk, seq_len=L)
    return pl.pallas_call(
        kern,
        grid=(Bsz, nqb),
        in_specs=[
            pl.BlockSpec((None, None, IDX_DIM, IDX_HEADS * QB), lambda b, q: (b, q, 0, 0)),
            pl.BlockSpec((None, IDX_HEADS, QB), lambda b, q: (b, 0, q)),
            pl.BlockSpec((None, nkb, KB, IDX_DIM), lambda b, q: (b, 0, 0, 0)),
            pl.BlockSpec((None, None, A_LATENT, A_HEADS * QB), lambda b, q: (b, q, 0, 0)),
            pl.BlockSpec((None, nkb, A_LATENT + A_ONES, KB), lambda b, q: (b, 0, 0, 0)),
            pl.BlockSpec((None, nkb, KB, A_LATENT), lambda b, q: (b, 0, 0, 0)),
            pl.BlockSpec((A_HEADS, A_V_DIM, A_LATENT), lambda b, q: (0, 0, 0)),
        ],
        out_specs=pl.BlockSpec((None, QB, A_HEADS * A_V_DIM), lambda b, q: (b, q, 0)),
        out_shape=jax.ShapeDtypeStruct((Bsz, L, A_HEADS * A_V_DIM), BF16),
        scratch_shapes=[
            pltpu.VMEM((nkb, KB, QB), F32),
            pltpu.VMEM((KB, A_HEADS * QB), F32),
            pltpu.VMEM((1, A_HEADS * QB), F32),
            pltpu.VMEM((A_LATENT + A_ONES, A_HEADS * QB), F32),
            pltpu.VMEM((A_HEADS * A_V_DIM, QB), F32),
            pltpu.VMEM((1, QB), jnp.int32),
        ],
        compiler_params=_cparams(("parallel", "parallel")),
        name="dsa_attn",
    )(qi, wi, ki, qlat, ct, c, wuvt)


def _outproj_ln_kernel(o_ref, w_ref, x_ref, g_ref, b_ref, y_ref):
    h = _dot(o_ref[...], w_ref[...])
    y_ref[...] = _layer_norm(DN_ALPHA * x_ref[...] + h, g_ref[...], b_ref[...])


def _outproj_ln(o, w_out, x, g, b):
    N, K = o.shape
    D = x.shape[1]
    T = ROW_TILE
    return pl.pallas_call(
        _outproj_ln_kernel,
        grid=(N // T,),
        in_specs=[
            pl.BlockSpec((T, K), lambda i: (i, 0)),
            pl.BlockSpec((K, D), lambda i: (0, 0)),
            pl.BlockSpec((T, D), lambda i: (i, 0)),
            pl.BlockSpec((1, D), lambda i: (0, 0)),
            pl.BlockSpec((1, D), lambda i: (0, 0)),
        ],
        out_specs=pl.BlockSpec((T, D), lambda i: (i, 0)),
        out_shape=jax.ShapeDtypeStruct((N, D), F32),
        compiler_params=_cparams(("parallel",)),
        name="outproj_ln",
    )(o, w_out.astype(BF16), x, g.reshape(1, D), b.reshape(1, D))


def _gla_proj_kernel(x_ref, wq_ref, wk_ref, wv_ref, wg_ref, wr_ref, wg2_ref, gb_ref,
                     q_ref, k_ref, v_ref, la_ref, r_ref, *, kh):
    xb = x_ref[...].astype(BF16)
    q_ref[...] = _dot(xb, wq_ref[...]) * (kh ** -0.5)
    k_ref[...] = _dot(xb, wk_ref[...])
    v_ref[...] = _dot(xb, wv_ref[...]).astype(BF16)
    r_ref[...] = _dot(xb, wr_ref[...])
    glr = _dot(xb, wg_ref[...])
    z = _dot(glr.astype(BF16), wg2_ref[...]) + gb_ref[...]
    log_sig = jnp.minimum(z, 0.0) - jnp.log(1.0 + jnp.exp(-jnp.abs(z)))
    la_ref[...] = log_sig / B_GATE_TAU


def _gla_proj(x, w_in, w_g2, g_bias):
    N, D = x.shape
    DK = w_g2.shape[1]
    DV = D
    T = ROW_TILE
    wb = w_in.astype(BF16)
    o1, o2, o3, o4 = DK, 2 * DK, 2 * DK + DV, 2 * DK + DV + B_GATE_RANK
    wq, wk, wv, wg, wr = wb[:, :o1], wb[:, o1:o2], wb[:, o2:o3], wb[:, o3:o4], wb[:, o4:]
    full = lambda shape: pl.BlockSpec(shape, lambda i: (0,) * len(shape))
    row = lambda w: pl.BlockSpec((T, w), lambda i: (i, 0))
    kern = functools.partial(_gla_proj_kernel, kh=DK // B_HEADS)
    return pl.pallas_call(
        kern,
        grid=(N // T,),
        in_specs=[row(D), full(wq.shape), full(wk.shape), full(wv.shape), full(wg.shape),
                  full(wr.shape), full(w_g2.shape), full((1, DK))],
        out_specs=[row(DK), row(DK), row(DV), row(DK), row(DV)],
        out_shape=[
            jax.ShapeDtypeStruct((N, DK), F32),
            jax.ShapeDtypeStruct((N, DK), F32),
            jax.ShapeDtypeStruct((N, DV), BF16),
            jax.ShapeDtypeStruct((N, DK), F32),
            jax.ShapeDtypeStruct((N, DV), F32),
        ],
        compiler_params=_cparams(("parallel",)),
        name="gla_proj",
    )(x, wq, wk, wv, wg, wr, w_g2.astype(BF16), g_bias.reshape(1, DK))


def _split3(a):
    hi = a.astype(BF16)
    r1 = a - hi.astype(F32)
    mid = r1.astype(BF16)
    lo = (r1 - mid.astype(F32)).astype(BF16)
    return hi, mid, lo


def _gla_kernel(q_ref, k_ref, v_ref, la_ref, r_ref, ng_ref, o_ref, st_ref, a_ref):
    C, SB = B_CHUNK, B_SUB
    nsub = C // SB
    kh = q_ref.shape[1] // B_HEADS
    vh = v_ref.shape[1] // B_HEADS

    @pl.when(pl.program_id(1) == 0)
    def _():
        st_ref[...] = jnp.zeros(st_ref.shape, F32)

    ri = lax.broadcasted_iota(jnp.int32, (C, C), 0)
    ci = lax.broadcasted_iota(jnp.int32, (C, C), 1)
    tril = jnp.where(ci <= ri, 1.0, 0.0).astype(BF16)
    sub_r = lax.broadcasted_iota(jnp.int32, (SB, 1), 0)
    sub_c = lax.broadcasted_iota(jnp.int32, (1, SB), 1)

    def head_chunk(rows, h):
        kc = slice(h * kh, (h + 1) * kh)
        vc = slice(h * vh, (h + 1) * vh)
        q = q_ref[rows, kc]
        k = k_ref[rows, kc]
        v = v_ref[rows, vc]
        la = la_ref[rows, kc]
        hi, mid, lo = _split3(la)
        b = _dot(tril, hi) + _dot(tril, mid) + _dot(tril, lo)
        st = st_ref[h]
        o = _dot_nt((q * jnp.exp(b)).astype(BF16), st.astype(BF16))

        a_ref[h] = jnp.zeros((C, C), F32)
        for I in range(nsub):
            r0 = I * SB
            bI = b[r0:r0 + SB, :]
            qI = q[r0:r0 + SB, :]
            kI = k[r0:r0 + SB, :]
            if I > 0:
                ref_lvl = b[r0 - 1:r0, :]
                qs = (qI * jnp.exp(bI - ref_lvl)).astype(BF16)
                ks = (k[:r0, :] * jnp.exp(ref_lvl - b[:r0, :])).astype(BF16)
                a_ref[h, r0:r0 + SB, :r0] = _dot_nt(qs, ks)
            diag = jnp.zeros((SB, SB), F32)
            for j in range(SB):
                dlt = jnp.where(sub_r >= j, bI - bI[j:j + 1, :], -jnp.inf)
                col = jnp.sum(qI * kI[j:j + 1, :] * jnp.exp(dlt), axis=1, keepdims=True)
                diag = diag + col * jnp.where(sub_c == j, 1.0, 0.0)
            a_ref[h, r0:r0 + SB, r0:r0 + SB] = diag
        o = o + _dot(a_ref[h].astype(BF16), v)

        b_last = b[C - 1:C, :]
        kd = (k * jnp.exp(b_last - b)).astype(BF16)
        st_ref[h] = st * jnp.exp(b_last) + _dot_tn(v, kd)

        o = o * lax.rsqrt(jnp.mean(o * o, axis=-1, keepdims=True) + RMS_EPS) * ng_ref[:, vc]
        r = r_ref[rows, vc]
        o_ref[rows, vc] = (o * (r / (1.0 + jnp.exp(-r)))).astype(BF16)

    def chunk_body(ch, carry):
        rows = pl.ds(pl.multiple_of(ch * C, C), C)
        for h in range(B_HEADS):
            head_chunk(rows, h)
        return carry

    lax.fori_loop(0, q_ref.shape[0] // C, chunk_body, 0)


def _gla_core(q, k, v, la, r, norm_g):
    Bsz, L, DK = q.shape
    DV = v.shape[2]
    kh, vh = DK // B_HEADS, DV // B_HEADS
    T = B_TBLK
    kspec = pl.BlockSpec((None, T, DK), lambda b, t: (b, t, 0))
    vspec = pl.BlockSpec((None, T, DV), lambda b, t: (b, t, 0))
    return pl.pallas_call(
        _gla_kernel,
        grid=(Bsz, L // T),
        in_specs=[kspec, kspec, vspec, kspec, vspec, pl.BlockSpec((1, DV), lambda b, t: (0, 0))],
        out_specs=vspec,
        out_shape=jax.ShapeDtypeStruct((Bsz, L, DV), BF16),
        scratch_shapes=[pltpu.VMEM((B_HEADS, vh, kh), F32),
                        pltpu.VMEM((B_HEADS, B_CHUNK, B_CHUNK), F32)],
        compiler_params=_cparams(("parallel", "arbitrary")),
        name="gla_core",
    )(q, k, v, la, r, norm_g.reshape(1, DV))


def _router_kernel(x_ref, w_ref, b_ref, idx_ref, gate_ref, rank_ref, cnt_ref, carry_ref):
    T = x_ref.shape[0]
    E = N_EXPERTS

    @pl.when(pl.program_id(0) == 0)
    def _():
        carry_ref[...] = jnp.zeros(carry_ref.shape, F32)

    x = x_ref[...]
    w = w_ref[...]
    xh = x.astype(BF16)
    xl = (x - xh.astype(F32)).astype(BF16)
    wh = w.astype(BF16)
    wl = (w - wh.astype(F32)).astype(BF16)
    logits = _dot(xh, wh) + (_dot(xh, wl) + _dot(xl, wh)) + b_ref[...]

    eidx = lax.broadcasted_iota(jnp.int32, (T, E), 1)
    work = logits
    vals, idxs = [], []
    for _ in range(TOP_K):
        mx = jnp.max(work, axis=1, keepdims=True)
        ix = jnp.min(jnp.where(work == mx, eidx, E), axis=1, keepdims=True)
        vals.append(mx)
        idxs.append(ix)
        work = jnp.where(eidx == ix, -jnp.inf, work)
    exps = [jnp.exp(vv - vals[0]) for vv in vals]
    den = exps[0] + exps[1] + exps[2] + exps[3]

    onehot = jnp.zeros((T, E), F32)
    for ix in idxs:
        onehot = onehot + jnp.where(eidx == ix, 1.0, 0.0)
    ri = lax.broadcasted_iota(jnp.int32, (T, T), 0)
    ci = lax.broadcasted_iota(jnp.int32, (T, T), 1)
    strict = jnp.where(ci < ri, 1.0, 0.0).astype(BF16)
    prefix = _dot(strict, onehot.astype(BF16)) + carry_ref[...]
    carry_ref[...] = carry_ref[...] + jnp.sum(onehot, axis=0, keepdims=True)
    cnt_ref[...] = carry_ref[...]

    for j in range(TOP_K):
        idx_ref[:, j:j + 1] = idxs[j]
        gate_ref[:, j:j + 1] = exps[j] / den
        rk = jnp.sum(jnp.where(eidx == idxs[j], prefix, 0.0), axis=1, keepdims=True)
        rank_ref[:, j:j + 1] = rk.astype(jnp.int32)


def _router(xt, w_router, b_router):
    N, D = xt.shape
    T = MOE_TOK_TILE
    tok = pl.BlockSpec((T, TOP_K), lambda i: (i, 0))
    return pl.pallas_call(
        _router_kernel,
        grid=(N // T,),
        in_specs=[
            pl.BlockSpec((T, D), lambda i: (i, 0)),
            pl.BlockSpec((D, N_EXPERTS), lambda i: (0, 0)),
            pl.BlockSpec((1, N_EXPERTS), lambda i: (0, 0)),
        ],
        out_specs=[tok, tok, tok, pl.BlockSpec((1, N_EXPERTS), lambda i: (0, 0))],
        out_shape=[
            jax.ShapeDtypeStruct((N, TOP_K), jnp.int32),
            jax.ShapeDtypeStruct((N, TOP_K), F32),
            jax.ShapeDtypeStruct((N, TOP_K), jnp.int32),
            jax.ShapeDtypeStruct((1, N_EXPERTS), F32),
        ],
        scratch_shapes=[pltpu.VMEM((1, N_EXPERTS), F32)],
        compiler_params=_cparams(("arbitrary",)),
        name="moe_router",
    )(xt, w_router, b_router.reshape(1, N_EXPERTS))


def _rowmap_kernel(dest_ref, init_ref, map_ref, sem, *, n_tok):
    i = pl.program_id(0)
    toks = dest_ref.shape[0] // TOP_K

    @pl.when(i == 0)
    def _():
        cp = pltpu.make_async_copy(init_ref, map_ref, sem)
        cp.start()
        cp.wait()

    t0 = i * toks

    def body(a, c):
        for j in range(TOP_K):
            map_ref[dest_ref[a * TOP_K + j]] = j * n_tok + t0 + a
        return c

    lax.fori_loop(0, toks, body, 0, unroll=2)


def _rowmap(dest_flat, n_tok, n_blocks):
    NK = dest_flat.shape[0]
    tile = min(NK, 8192)
    P = n_blocks * MOE_BLOCK
    init = n_tok * TOP_K + (jnp.arange(P, dtype=jnp.int32) & (MOE_BLOCK - 1))
    return pl.pallas_call(
        functools.partial(_rowmap_kernel, n_tok=n_tok),
        grid=(NK // tile,),
        in_specs=[pl.BlockSpec((tile,), lambda i: (i,), memory_space=pltpu.SMEM),
                  pl.BlockSpec(memory_space=pl.ANY)],
        out_specs=pl.BlockSpec((P,), lambda i: (0,), memory_space=pltpu.SMEM),
        out_shape=jax.ShapeDtypeStruct((P,), jnp.int32),
        scratch_shapes=[pltpu.SemaphoreType.DMA],
        compiler_params=_cparams(("arbitrary",)),
        name="moe_rowmap",
    )(dest_flat, init)


def _expert_kernel(blk_e_ref, nused_ref, map_prev_ref, map_cur_ref, map_next_ref,
                   x_ref, w1_ref, b1_ref, w2_ref, b2_ref, y4_ref,
                   xbuf0, xbuf1, ybuf0, ybuf1, act_ref, w1b_ref, w2b_ref, gsem, ssem, *, n_tok):
    i = pl.program_id(0)
    n_used = nused_ref[0]
    BLK = MOE_BLOCK
    spare_row0 = n_tok * TOP_K

    def gather_row(map_ref, r, xdst):
        tok = map_ref[0, r] & (n_tok - 1)
        return pltpu.make_async_copy(x_ref.at[pl.ds(tok, 1)], xdst.at[pl.ds(r, 1)], gsem)

    def scatter_row(dst, r, ysrc):
        return pltpu.make_async_copy(ysrc.at[pl.ds(r, 1)], y4_ref.at[pl.ds(dst, 1)], ssem)

    def wait_rows(sem):
        def body(r, c):
            pltpu.make_async_copy(xbuf0.at[pl.ds(r, 1)], xbuf1.at[pl.ds(r, 1)], sem).wait()
            return c
        lax.fori_loop(0, BLK, body, 0, unroll=8)

    @pl.when(i == 0)
    def _():
        ybuf1[...] = jnp.zeros(ybuf1.shape, F32)

        def first(r, c):
            gather_row(map_cur_ref, r, xbuf0).start()
            return c
        lax.fori_loop(0, BLK, first, 0, unroll=8)

    prev = blk_e_ref[jnp.maximum(i - 1, 0)]
    fresh = (i == 0) | (blk_e_ref[i] != prev)

    @pl.when(fresh & (i < n_used))
    def _():
        w1b_ref[...] = w1_ref[...].astype(BF16)
        w2b_ref[...] = w2_ref[...].astype(BF16)

    def step(xcur, xnext, ycur, yprev):
        wait_rows(gsem)

        @pl.when(i >= 1)
        def _():
            wait_rows(ssem)

        F = w2_ref.shape[0]
        D = w2_ref.shape[1]
        n1, n2 = F // MOE_CHUNK, D // MOE_CHUNK
        rows_per_group = BLK // (n1 + n2)

        def neighbour_dmas(group):
            for r in range(group * rows_per_group, (group + 1) * rows_per_group):
                gather_row(map_next_ref, r, xnext).start()
                dst = jnp.where(i == 0, spare_row0 + r, map_prev_ref[0, r])
                scatter_row(dst, r, yprev).start()

        xb = xcur[...].astype(BF16)
        for c in range(n1):
            gc = slice(c * MOE_CHUNK, (c + 1) * MOE_CHUNK)
            uc = slice(F + c * MOE_CHUNK, F + (c + 1) * MOE_CHUNK)
            g = jnp.minimum(_dot(xb, w1b_ref[:, gc]) + b1_ref[:, gc], SWIGLU_LIMIT)
            u = jnp.clip(_dot(xb, w1b_ref[:, uc]) + b1_ref[:, uc], -SWIGLU_LIMIT, SWIGLU_LIMIT)
            glu = g / (1.0 + jnp.exp(-SWIGLU_ALPHA * g))
            act_ref[:, gc] = ((u + 1.0) * glu).astype(BF16)
            neighbour_dmas(c)
        act = act_ref[...]
        for c in range(n2):
            oc = slice(c * MOE_CHUNK, (c + 1) * MOE_CHUNK)
            ycur[:, oc] = _dot(act, w2b_ref[:, oc]) + b2_ref[:, oc]
            neighbour_dmas(n1 + c)

        @pl.when(i == n_used - 1)
        def _():
            wait_rows(gsem)
            wait_rows(ssem)

            def last(r, c):
                scatter_row(map_cur_ref[0, r], r, ycur).start()
                return c
            lax.fori_loop(0, BLK, last, 0, unroll=8)
            wait_rows(ssem)

    @pl.when((i < n_used) & (i % 2 == 0))
    def _():
        step(xbuf0, xbuf1, ybuf0, ybuf1)

    @pl.when((i < n_used) & (i % 2 == 1))
    def _():
        step(xbuf1, xbuf0, ybuf1, ybuf0)


def _experts(x, row_map, blk_e, n_used, w1, b1, w2, b2, layer):
    N, D = x.shape
    assert N & (N - 1) == 0, "token count must be a power of two (row map decode)"
    nb = row_map.shape[0] // MOE_BLOCK
    _, E, _, F2 = w1.shape
    F = w2.shape[2]
    map3 = row_map.reshape(nb, 1, MOE_BLOCK)
    wspec = lambda r, c: pl.BlockSpec((None, None, r, c), lambda i, be, nu: (layer, be[i], 0, 0))
    mspec = lambda off: pl.BlockSpec(
        (None, 1, MOE_BLOCK), lambda i, be, nu: (jnp.clip(i + off, 0, nb - 1), 0, 0),
        memory_space=pltpu.SMEM)
    grid_spec = pltpu.PrefetchScalarGridSpec(
        num_scalar_prefetch=2,
        grid=(nb,),
        in_specs=[
            mspec(-1), mspec(0), mspec(1),
            pl.BlockSpec(memory_space=pl.ANY),
            wspec(D, F2), wspec(1, F2), wspec(F, D), wspec(1, D),
        ],
        out_specs=pl.BlockSpec(memory_space=pl.ANY),
        scratch_shapes=[pltpu.VMEM((MOE_BLOCK, D), F32), pltpu.VMEM((MOE_BLOCK, D), F32),
                        pltpu.VMEM((MOE_BLOCK, D), F32), pltpu.VMEM((MOE_BLOCK, D), F32),
                        pltpu.VMEM((MOE_BLOCK, F), BF16),
                        pltpu.VMEM((D, F2), BF16), pltpu.VMEM((F, D), BF16),
                        pltpu.SemaphoreType.DMA, pltpu.SemaphoreType.DMA],
    )
    nl = w1.shape[0]
    return pl.pallas_call(
        functools.partial(_expert_kernel, n_tok=N),
        grid_spec=grid_spec,
        out_shape=jax.ShapeDtypeStruct((N * TOP_K + MOE_BLOCK, D), F32),
        compiler_params=_cparams(("arbitrary",)),
        name="moe_experts",
    )(blk_e, n_used, map3, map3, map3, x, w1, b1.reshape(nl, E, 1, F2), w2, b2.reshape(nl, E, 1, D))


def _combine_ln_kernel(gate_ref, x_ref, y0_ref, y1_ref, y2_ref, y3_ref, g_ref, b_ref, o_ref):
    gate = gate_ref[...]
    f = jnp.zeros(x_ref.shape, F32)
    for j, y_ref in enumerate((y0_ref, y1_ref, y2_ref, y3_ref)):
        f = f + y_ref[...] * gate[:, j:j + 1]
    o_ref[...] = _layer_norm(DN_ALPHA * x_ref[...] + f, g_ref[...], b_ref[...])


def _combine_ln(xt, y4, gates, g, b):
    N, D = xt.shape
    T = MOE_TOK_TILE
    nt = N // T
    yspec = lambda j: pl.BlockSpec((T, D), lambda i: (j * nt + i, 0))
    return pl.pallas_call(
        _combine_ln_kernel,
        grid=(nt,),
        in_specs=[
            pl.BlockSpec((T, TOP_K), lambda i: (i, 0)),
            pl.BlockSpec((T, D), lambda i: (i, 0)),
            yspec(0), yspec(1), yspec(2), yspec(3),
            pl.BlockSpec((1, D), lambda i: (0, 0)),
            pl.BlockSpec((1, D), lambda i: (0, 0)),
        ],
        out_specs=pl.BlockSpec((T, D), lambda i: (i, 0)),
        out_shape=jax.ShapeDtypeStruct((N, D), F32),
        compiler_params=_cparams(("parallel",)),
        name="moe_combine_ln",
    )(gates, xt, y4, y4, y4, y4, g.reshape(1, D), b.reshape(1, D))


def _moe_ln(xt, w_router, b_router, w1, b1, w2, b2, g, b, layer):
    N, D = xt.shape
    idx, gates, rank, counts = _router(xt, w_router, b_router)
    counts = counts.reshape(N_EXPERTS).astype(jnp.int32)
    padded = ((counts + MOE_BLOCK - 1) // MOE_BLOCK) * MOE_BLOCK
    pend = jnp.cumsum(padded)
    pstart = pend - padded
    n_blocks = -(-(N * TOP_K) // MOE_BLOCK) + N_EXPERTS
    dest = (pstart[idx] + rank).reshape(N * TOP_K)
    blk_start = jnp.arange(n_blocks, dtype=jnp.int32) * MOE_BLOCK
    blk_e = jnp.minimum(jnp.sum((pend[None, :] <= blk_start[:, None]).astype(jnp.int32), axis=1),
                        N_EXPERTS - 1)
    n_used = (pend[-1:] // MOE_BLOCK).astype(jnp.int32)
    row_map = _rowmap(dest, N, n_blocks)
    y4 = _experts(xt, row_map, blk_e, n_used, w1, b1, w2, b2, layer)
    return _combine_ln(xt, y4, gates, g, b)


def _dsa_layer(x, w_in, kv_norm, w_uk, w_uv, w_out, g, b):
    Bsz, L, D = x.shape
    qlat, c, ct, qi, ki, wi = _dsa_proj(x, w_in, kv_norm, w_uk)
    o = _dsa_attention(qi, wi, ki, qlat, ct, c, w_uv)
    return _outproj_ln(o.reshape(Bsz * L, -1), w_out, x.reshape(Bsz * L, D), g, b)


def _gla_layer(x, w_in, w_g2, g_bias, norm_g, w_out, g, b):
    Bsz, L, D = x.shape
    xt = x.reshape(Bsz * L, D)
    q, k, v, la, r = _gla_proj(xt, w_in, w_g2, g_bias)
    sh = lambda t: t.reshape(Bsz, L, t.shape[-1])
    o = _gla_core(sh(q), sh(k), sh(v), sh(la), sh(r), norm_g)
    return _outproj_ln(o.reshape(Bsz * L, -1), w_out, xt, g, b)


def kernel(x, a_w_in, a_kv_norm, a_w_uk, a_w_uv, a_w_out, b_w_in, b_w_g2, b_g_bias, b_norm, b_w_out, m_w_router, m_b_router, m_w1, m_b1, m_w2, m_b2, ln1_g, ln1_b, ln2_g, ln2_b):
    Bsz, L, D = x.shape
    for i in range(DEPTH):
        j = i // 2
        if i % 2 == 0:
            xt = _dsa_layer(x, a_w_in[j], a_kv_norm[j], a_w_uk[j], a_w_uv[j], a_w_out[j],
                            ln1_g[i], ln1_b[i])
        else:
            xt = _gla_layer(x, b_w_in[j], b_w_g2[j], b_g_bias[j], b_norm[j], b_w_out[j],
                            ln1_g[i], ln1_b[i])
        xt = _moe_ln(xt, m_w_router[i], m_b_router[i], m_w1, m_b1, m_w2, m_b2,
                     ln2_g[i], ln2_b[i], i)
        x = xt.reshape(Bsz, L, D)
    return x
```

```python
import functools

import jax
import jax.numpy as jnp
from jax import lax
from jax.experimental import pallas as pl
from jax.experimental.pallas import tpu as pltpu

F32 = jnp.float32
BF16 = jnp.bfloat16

DEPTH = 2
DN_ALPHA = (2.0 * DEPTH) ** 0.25
LN_EPS = 1e-5
RMS_EPS = 1e-6

A_HEADS = 16
A_QK_DIM = 64
A_V_DIM = 64
A_LATENT = 256
IDX_HEADS = 8
IDX_DIM = 64
IDX_TOPK_MAX = 256
A_QBLK = 128
A_KBLK = 512
A_ONES = 16
CNT_ROWS = 32
BIS_STEPS = 4
BIS_MAX_STEPS = 36
BIS_FIRST_PIVOT = 0.0625
LOG2E = 1.4426950408889634

B_HEADS = 4
B_GATE_RANK = 16
B_GATE_TAU = 16.0
B_CHUNK = 64
B_SUB = 16
B_TBLK = 512

N_EXPERTS = 32
TOP_K = 4
SWIGLU_LIMIT = 7.0
SWIGLU_ALPHA = 1.702
MOE_BLOCK = 256
MOE_CHUNK = 256
ROW_TILE = 512
MOE_TOK_TILE = 256

VMEM_LIMIT = 56 * 1024 * 1024
NEG_BIG = -1e30


def _cparams(sem):
    return pltpu.CompilerParams(dimension_semantics=sem, vmem_limit_bytes=VMEM_LIMIT)


def _dot(a, b):
    return jnp.dot(a, b, preferred_element_type=F32)


def _dot_nt(a, b):
    return lax.dot_general(a, b, (((1,), (1,)), ((), ())), preferred_element_type=F32)


def _dot_tn(a, b):
    return lax.dot_general(a, b, (((0,), (0,)), ((), ())), preferred_element_type=F32)


def _layer_norm(z, g, b):
    mu = jnp.mean(z, axis=-1, keepdims=True)
    zc = z - mu
    var = jnp.mean(zc * zc, axis=-1, keepdims=True)
    return zc * lax.rsqrt(var + LN_EPS) * g + b


def _dsa_proj_kernel(x_ref, wqt_ref, wc_ref, wqit_ref, wki_ref, wwit_ref, kvn_ref, wukt_ref,
                     qlat_ref, c_ref, ct_ref, qi_ref, ki_ref, wi_ref):
    T = x_ref.shape[0]
    QB = A_QBLK
    xb = x_ref[...].astype(BF16)
    qt = _dot_nt(wqt_ref[...], xb)
    for h in range(A_HEADS):
        qh = qt[h * A_QK_DIM:(h + 1) * A_QK_DIM, :].astype(BF16)
        ql = (_dot(wukt_ref[h], qh) * (A_QK_DIM ** -0.5 * LOG2E)).astype(BF16)
        for j in range(T // QB):
            qlat_ref[j, :, h * QB:(h + 1) * QB] = ql[:, j * QB:(j + 1) * QB]
    c = _dot(xb, wc_ref[...])
    c = c * lax.rsqrt(jnp.mean(c * c, axis=-1, keepdims=True) + RMS_EPS) * kvn_ref[...]
    c_ref[...] = c.astype(BF16)
    ct_ref[:A_LATENT, :] = c.T.astype(BF16)
    ct_ref[A_LATENT:, :] = jnp.ones((A_ONES, T), BF16)
    qit = _dot_nt(wqit_ref[...], xb) * (IDX_DIM ** -0.5)
    for h in range(IDX_HEADS):
        qh = qit[h * IDX_DIM:(h + 1) * IDX_DIM, :].astype(BF16)
        for j in range(T // QB):
            qi_ref[j, :, h * QB:(h + 1) * QB] = qh[:, j * QB:(j + 1) * QB]
    ki_ref[...] = _dot(xb, wki_ref[...]).astype(BF16)
    wi_ref[...] = _dot_nt(wwit_ref[...], xb) * (IDX_HEADS ** -0.5)


def _dsa_proj(x, w_in, kv_norm, w_uk):
    Bsz, L, D = x.shape
    T, QB = A_KBLK, A_QBLK
    nkb, nqb, qpt = L // T, L // QB, T // QB
    o1 = A_HEADS * A_QK_DIM
    o2 = o1 + A_LATENT
    o3 = o2 + IDX_HEADS * IDX_DIM
    o4 = o3 + IDX_DIM
    wb = w_in.astype(BF16)
    wqt, wc, wqit, wki, wwit = wb[:, :o1].T, wb[:, o1:o2], wb[:, o2:o3].T, wb[:, o3:o4], wb[:, o4:].T
    wukt = jnp.swapaxes(w_uk, 1, 2).astype(BF16)
    full = lambda shape: pl.BlockSpec(shape, lambda b, t: (0,) * len(shape))
    return pl.pallas_call(
        _dsa_proj_kernel,
        grid=(Bsz, nkb),
        in_specs=[
            pl.BlockSpec((None, T, D), lambda b, t: (b, t, 0)),
            full(wqt.shape), full(wc.shape), full(wqit.shape), full(wki.shape), full(wwit.shape),
            full((1, A_LATENT)), full(wukt.shape),
        ],
        out_specs=[
            pl.BlockSpec((None, qpt, A_LATENT, A_HEADS * QB), lambda b, t: (b, t, 0, 0)),
            pl.BlockSpec((None, None, T, A_LATENT), lambda b, t: (b, t, 0, 0)),
            pl.BlockSpec((None, None, A_LATENT + A_ONES, T), lambda b, t: (b, t, 0, 0)),
            pl.BlockSpec((None, qpt, IDX_DIM, IDX_HEADS * QB), lambda b, t: (b, t, 0, 0)),
            pl.BlockSpec((None, None, T, IDX_DIM), lambda b, t: (b, t, 0, 0)),
            pl.BlockSpec((None, IDX_HEADS, T), lambda b, t: (b, 0, t)),
        ],
        out_shape=[
            jax.ShapeDtypeStruct((Bsz, nqb, A_LATENT, A_HEADS * QB), BF16),
            jax.ShapeDtypeStruct((Bsz, nkb, T, A_LATENT), BF16),
            jax.ShapeDtypeStruct((Bsz, nkb, A_LATENT + A_ONES, T), BF16),
            jax.ShapeDtypeStruct((Bsz, nqb, IDX_DIM, IDX_HEADS * QB), BF16),
            jax.ShapeDtypeStruct((Bsz, nkb, T, IDX_DIM), BF16),
            jax.ShapeDtypeStruct((Bsz, IDX_HEADS, L), F32),
        ],
        compiler_params=_cparams(("parallel", "parallel")),
        name="dsa_proj",
    )(x, wqt, wc, wqit, wki, wwit, kv_norm.reshape(1, A_LATENT), wukt)


def _ordered_int_to_float(o):
    bits = jnp.where(o >= 0, o, o ^ jnp.int32(2 ** 31 - 1))
    return lax.bitcast_convert_type(bits, F32)


def _float_to_ordered_int(x):
    bits = lax.bitcast_convert_type(jnp.asarray(x, F32), jnp.int32)
    return jnp.where(bits >= 0, bits, bits ^ jnp.int32(2 ** 31 - 1))


def _dsa_attn_kernel(qi_ref, wi_ref, ki_ref, qlat_ref, ct_ref, c_ref, wuvt_ref, o_ref,
                     sc_ref, s_ref, m_ref, acc_ref, ot_ref, jcut_ref, *, topk, seq_len):
    QB, KB = A_QBLK, A_KBLK
    qb = pl.program_id(1)
    q0 = qb * QB
    nvb = (q0 + QB + KB - 1) // KB
    qpos = q0 + lax.broadcasted_iota(jnp.int32, (1, QB), 1)
    krow = lax.broadcasted_iota(jnp.int32, (KB, QB), 0)
    kf = float(topk)

    wi = wi_ref[...]
    qi = qi_ref[...]

    def score_body(kb, carry):
        s = _dot(ki_ref[kb], qi)
        sc = jnp.zeros((KB, QB), F32)
        for h in range(IDX_HEADS):
            sc = sc + jnp.maximum(s[:, h * QB:(h + 1) * QB], 0.0) * wi[h:h + 1, :]
        sc_ref[kb] = jnp.where((kb * KB + krow) <= qpos, sc, -jnp.inf)
        return carry

    lax.fori_loop(0, nvb, score_body, 0)

    def count(pred):
        def body(kb, acc):
            hit = jnp.where(pred(sc_ref[kb], kb), 1.0, 0.0)
            return acc + jnp.sum(hit.reshape(KB // CNT_ROWS, CNT_ROWS, QB), axis=0)
        acc = lax.fori_loop(0, nvb, body, jnp.zeros((CNT_ROWS, QB), F32))
        return jnp.sum(acc, axis=0, keepdims=True)

    def row_max():
        def body(kb, acc):
            return jnp.maximum(acc, jnp.max(sc_ref[kb].reshape(KB // CNT_ROWS, CNT_ROWS, QB), axis=0))
        acc = lax.fori_loop(0, nvb, body, jnp.full((CNT_ROWS, QB), -jnp.inf, F32))
        return jnp.max(acc, axis=0, keepdims=True)

    def half_gap(lo, hi):
        return lax.shift_right_logical(hi - lo, jnp.int32(1))

    def bis_step(st, mid):
        lo, hi, thr, done = st
        cand = _ordered_int_to_float(mid)
        cnt = count(lambda blk, kb: blk >= cand)
        ge = cnt >= kf
        hit = (cnt == kf) & (done < 0.5)
        return (jnp.where(ge, mid, lo), jnp.where(ge, hi, mid),
                jnp.where(hit, cand, thr), jnp.where(hit, 1.0, done))

    def bis_cond(st):
        i, lo, hi, _, done = st
        open_ = jnp.where((done < 0.5) & (half_gap(lo, hi) > 0), 1.0, 0.0)
        return (i < BIS_MAX_STEPS) & (jnp.max(open_) > 0.5)

    def bis_body(st):
        i, st = st[0], st[1:]
        for _ in range(BIS_STEPS):
            st = bis_step(st, st[0] + half_gap(st[0], st[1]))
        return (i + BIS_STEPS,) + st

    smax = row_max()
    lo0 = jnp.full((1, QB), _float_to_ordered_int(-jnp.inf), jnp.int32)
    hi0 = _float_to_ordered_int(smax) + 1
    done0 = jnp.where(qpos < topk, 1.0, 0.0)
    st = (lo0, hi0, jnp.full((1, QB), -jnp.inf, F32), done0)
    first = jnp.where(smax > 0.0, _float_to_ordered_int(smax * BIS_FIRST_PIVOT),
                      lo0 + half_gap(lo0, hi0))
    st = bis_step(st, first)
    _, u, _, thr, done = lax.while_loop(bis_cond, bis_body, (jnp.int32(0),) + st)
    thr = jnp.where(done > 0.5, thr, _ordered_int_to_float(u))
    jcut_ref[...] = jnp.full((1, QB), seq_len, jnp.int32)

    @pl.when(jnp.min(done) < 0.5)
    def _():
        need = kf - count(lambda blk, kb: blk > thr)
        nbits = max(1, (seq_len - 1).bit_length())

        def jbody(i, x):
            trial = x | jnp.left_shift(jnp.int32(1), nbits - 1 - i)
            g = count(lambda blk, kb: (blk == thr) & ((kb * KB + krow) < trial))
            return jnp.where(g < need, trial, x)

        x = lax.fori_loop(0, nbits, jbody, jnp.zeros((1, QB), jnp.int32))
        jcut_ref[...] = jnp.where(done > 0.5, seq_len, x)

    jcut = jcut_ref[...]

    m_ref[...] = jnp.full(m_ref.shape, NEG_BIG, F32)
    acc_ref[...] = jnp.zeros(acc_ref.shape, F32)

    pair_cols = [slice(hp * 2 * QB, (hp + 1) * 2 * QB) for hp in range(A_HEADS // 2)]

    def logits(kb, cols):
        return _dot(c_ref[kb], qlat_ref[:, cols])

    def mask_bias(kb):
        blk = sc_ref[kb]
        kpos = kb * KB + krow
        sel = ((blk > thr) | ((blk == thr) & (kpos <= jcut))) & (kpos <= qpos)
        bias = jnp.where(sel, 0.0, NEG_BIG)
        return jnp.concatenate([bias, bias], axis=1)

    def consume(kb, s, cols, bias2):
        s = s + bias2
        m_prev = m_ref[:, cols]
        m_new = jnp.maximum(m_prev, jnp.max(s, axis=0, keepdims=True))
        alpha = jnp.exp2(m_prev - m_new)
        p = jnp.exp2(s - m_new).astype(BF16)
        acc_ref[:, cols] = alpha * acc_ref[:, cols] + _dot(ct_ref[kb], p)
        m_ref[:, cols] = m_new

    for cols in pair_cols:
        s_ref[:, cols] = logits(0, cols)

    def attn_body(kb, carry):
        bias2 = mask_bias(kb)
        for cols in pair_cols:
            s = s_ref[:, cols]
            s_ref[:, cols] = logits(kb + 1, cols)
            consume(kb, s, cols, bias2)
        return carry

    lax.fori_loop(0, nvb - 1, attn_body, 0)
    bias2 = mask_bias(nvb - 1)
    for cols in pair_cols:
        consume(nvb - 1, s_ref[:, cols], cols, bias2)

    for h in range(A_HEADS):
        cols = slice(h * QB, (h + 1) * QB)
        oh = acc_ref[:A_LATENT, cols] / acc_ref[A_LATENT:A_LATENT + 1, cols]
        ot_ref[h * A_V_DIM:(h + 1) * A_V_DIM, :] = _dot(wuvt_ref[h], oh.astype(BF16))
    o_ref[...] = ot_ref[...].T.astype(BF16)


def _dsa_attention(qi, wi, ki, qlat, ct, c, w_uv):
    Bsz, nqb, _, _ = qlat.shape
    QB, KB = A_QBLK, A_KBLK
    L = nqb * QB
    nkb = L // KB
    topk = min(IDX_TOPK_MAX, L // 4)
    wuvt = jnp.swapaxes(w_uv, 1, 2).astype(BF16)
    kern = functools.partial(_dsa_attn_kernel, topk=topk, seq_len=L)
    return pl.pallas_call(
        kern,
        grid=(Bsz, nqb),
        in_specs=[
            pl.BlockSpec((None, None, IDX_DIM, IDX_HEADS * QB), lambda b, q: (b, q, 0, 0)),
            pl.BlockSpec((None, IDX_HEADS, QB), lambda b, q: (b, 0, q)),
            pl.BlockSpec((None, nkb, KB, IDX_DIM), lambda b, q: (b, 0, 0, 0)),
            pl.BlockSpec((None, None, A_LATENT, A_HEADS * QB), lambda b, q: (b, q, 0, 0)),
            pl.BlockSpec((None, nkb, A_LATENT + A_ONES, KB), lambda b, q: (b, 0, 0, 0)),
            pl.BlockSpec((None, nkb, KB, A_LATENT), lambda b, q: (b, 0, 0, 0)),
            pl.BlockSpec((A_HEADS, A_V_DIM, A_LATENT), lambda b, q: (0, 0, 0)),
        ],
        out_specs=pl.BlockSpec((None, QB, A_HEADS * A_V_DIM), lambda b, q: (b, q, 0)),
        out_shape=jax.ShapeDtypeStruct((Bsz, L, A_HEADS * A_V_DIM), BF16),
        scratch_shapes=[
            pltpu.VMEM((nkb, KB, QB), F32),
            pltpu.VMEM((KB, A_HEADS * QB), F32),
            pltpu.VMEM((1, A_HEADS * QB), F32),
            pltpu.VMEM((A_LATENT + A_ONES, A_HEADS * QB), F32),
            pltpu.VMEM((A_HEADS * A_V_DIM, QB), F32),
            pltpu.VMEM((1, QB), jnp.int32),
        ],
        compiler_params=_cparams(("parallel", "parallel")),
        name="dsa_attn",
    )(qi, wi, ki, qlat, ct, c, wuvt)


def _outproj_ln_kernel(o_ref, w_ref, x_ref, g_ref, b_ref, y_ref):
    h = _dot(o_ref[...], w_ref[...])
    y_ref[...] = _layer_norm(DN_ALPHA * x_ref[...] + h, g_ref[...], b_ref[...])


def _outproj_ln(o, w_out, x, g, b):
    N, K = o.shape
    D = x.shape[1]
    T = ROW_TILE
    return pl.pallas_call(
        _outproj_ln_kernel,
        grid=(N // T,),
        in_specs=[
            pl.BlockSpec((T, K), lambda i: (i, 0)),
            pl.BlockSpec((K, D), lambda i: (0, 0)),
            pl.BlockSpec((T, D), lambda i: (i, 0)),
            pl.BlockSpec((1, D), lambda i: (0, 0)),
            pl.BlockSpec((1, D), lambda i: (0, 0)),
        ],
        out_specs=pl.BlockSpec((T, D), lambda i: (i, 0)),
        out_shape=jax.ShapeDtypeStruct((N, D), F32),
        compiler_params=_cparams(("parallel",)),
        name="outproj_ln",
    )(o, w_out.astype(BF16), x, g.reshape(1, D), b.reshape(1, D))


def _gla_proj_kernel(x_ref, wq_ref, wk_ref, wv_ref, wg_ref, wr_ref, wg2_ref, gb_ref,
                     q_ref, k_ref, v_ref, la_ref, r_ref, *, kh):
    xb = x_ref[...].astype(BF16)
    q_ref[...] = _dot(xb, wq_ref[...]) * (kh ** -0.5)
    k_ref[...] = _dot(xb, wk_ref[...])
    v_ref[...] = _dot(xb, wv_ref[...]).astype(BF16)
    r_ref[...] = _dot(xb, wr_ref[...])
    glr = _dot(xb, wg_ref[...])
    z = _dot(glr.astype(BF16), wg2_ref[...]) + gb_ref[...]
    log_sig = jnp.minimum(z, 0.0) - jnp.log(1.0 + jnp.exp(-jnp.abs(z)))
    la_ref[...] = log_sig / B_GATE_TAU


def _gla_proj(x, w_in, w_g2, g_bias):
    N, D = x.shape
    DK = w_g2.shape[1]
    DV = D
    T = ROW_TILE
    wb = w_in.astype(BF16)
    o1, o2, o3, o4 = DK, 2 * DK, 2 * DK + DV, 2 * DK + DV + B_GATE_RANK
    wq, wk, wv, wg, wr = wb[:, :o1], wb[:, o1:o2], wb[:, o2:o3], wb[:, o3:o4], wb[:, o4:]
    full = lambda shape: pl.BlockSpec(shape, lambda i: (0,) * len(shape))
    row = lambda w: pl.BlockSpec((T, w), lambda i: (i, 0))
    kern = functools.partial(_gla_proj_kernel, kh=DK // B_HEADS)
    return pl.pallas_call(
        kern,
        grid=(N // T,),
        in_specs=[row(D), full(wq.shape), full(wk.shape), full(wv.shape), full(wg.shape),
                  full(wr.shape), full(w_g2.shape), full((1, DK))],
        out_specs=[row(DK), row(DK), row(DV), row(DK), row(DV)],
        out_shape=[
            jax.ShapeDtypeStruct((N, DK), F32),
            jax.ShapeDtypeStruct((N, DK), F32),
            jax.ShapeDtypeStruct((N, DV), BF16),
            jax.ShapeDtypeStruct((N, DK), F32),
            jax.ShapeDtypeStruct((N, DV), F32),
        ],
        compiler_params=_cparams(("parallel",)),
        name="gla_proj",
    )(x, wq, wk, wv, wg, wr, w_g2.astype(BF16), g_bias.reshape(1, DK))


def _split3(a):
    hi = a.astype(BF16)
    r1 = a - hi.astype(F32)
    mid = r1.astype(BF16)
    lo = (r1 - mid.astype(F32)).astype(BF16)
    return hi, mid, lo


def _gla_kernel(q_ref, k_ref, v_ref, la_ref, r_ref, ng_ref, o_ref, st_ref, a_ref):
    C, SB = B_CHUNK, B_SUB
    nsub = C // SB
    kh = q_ref.shape[1] // B_HEADS
    vh = v_ref.shape[1] // B_HEADS

    @pl.when(pl.program_id(1) == 0)
    def _():
        st_ref[...] = jnp.zeros(st_ref.shape, F32)

    ri = lax.broadcasted_iota(jnp.int32, (C, C), 0)
    ci = lax.broadcasted_iota(jnp.int32, (C, C), 1)
    tril = jnp.where(ci <= ri, 1.0, 0.0).astype(BF16)
    sub_r = lax.broadcasted_iota(jnp.int32, (SB, 1), 0)
    sub_c = lax.broadcasted_iota(jnp.int32, (1, SB), 1)

    def head_chunk(rows, h):
        kc = slice(h * kh, (h + 1) * kh)
        vc = slice(h * vh, (h + 1) * vh)
        q = q_ref[rows, kc]
        k = k_ref[rows, kc]
        v = v_ref[rows, vc]
        la = la_ref[rows, kc]
        hi, mid, lo = _split3(la)
        b = _dot(tril, hi) + _dot(tril, mid) + _dot(tril, lo)
        st = st_ref[h]
        o = _dot_nt((q * jnp.exp(b)).astype(BF16), st.astype(BF16))

        a_ref[h] = jnp.zeros((C, C), F32)
        for I in range(nsub):
            r0 = I * SB
            bI = b[r0:r0 + SB, :]
            qI = q[r0:r0 + SB, :]
            kI = k[r0:r0 + SB, :]
            if I > 0:
                ref_lvl = b[r0 - 1:r0, :]
                qs = (qI * jnp.exp(bI - ref_lvl)).astype(BF16)
                ks = (k[:r0, :] * jnp.exp(ref_lvl - b[:r0, :])).astype(BF16)
                a_ref[h, r0:r0 + SB, :r0] = _dot_nt(qs, ks)
            diag = jnp.zeros((SB, SB), F32)
            for j in range(SB):
                dlt = jnp.where(sub_r >= j, bI - bI[j:j + 1, :], -jnp.inf)
                col = jnp.sum(qI * kI[j:j + 1, :] * jnp.exp(dlt), axis=1, keepdims=True)
                diag = diag + col * jnp.where(sub_c == j, 1.0, 0.0)
            a_ref[h, r0:r0 + SB, r0:r0 + SB] = diag
        o = o + _dot(a_ref[h].astype(BF16), v)

        b_last = b[C - 1:C, :]
        kd = (k * jnp.exp(b_last - b)).astype(BF16)
        st_ref[h] = st * jnp.exp(b_last) + _dot_tn(v, kd)

        o = o * lax.rsqrt(jnp.mean(o * o, axis=-1, keepdims=True) + RMS_EPS) * ng_ref[:, vc]
        r = r_ref[rows, vc]
        o_ref[rows, vc] = (o * (r / (1.0 + jnp.exp(-r)))).astype(BF16)

    def chunk_body(ch, carry):
        rows = pl.ds(pl.multiple_of(ch * C, C), C)
        for h in range(B_HEADS):
            head_chunk(rows, h)
        return carry

    lax.fori_loop(0, q_ref.shape[0] // C, chunk_body, 0)


def _gla_core(q, k, v, la, r, norm_g):
    Bsz, L, DK = q.shape
    DV = v.shape[2]
    kh, vh = DK // B_HEADS, DV // B_HEADS
    T = B_TBLK
    kspec = pl.BlockSpec((None, T, DK), lambda b, t: (b, t, 0))
    vspec = pl.BlockSpec((None, T, DV), lambda b, t: (b, t, 0))
    return pl.pallas_call(
        _gla_kernel,
        grid=(Bsz, L // T),
        in_specs=[kspec, kspec, vspec, kspec, vspec, pl.BlockSpec((1, DV), lambda b, t: (0, 0))],
        out_specs=vspec,
        out_shape=jax.ShapeDtypeStruct((Bsz, L, DV), BF16),
        scratch_shapes=[pltpu.VMEM((B_HEADS, vh, kh), F32),
                        pltpu.VMEM((B_HEADS, B_CHUNK, B_CHUNK), F32)],
        compiler_params=_cparams(("parallel", "arbitrary")),
        name="gla_core",
    )(q, k, v, la, r, norm_g.reshape(1, DV))


def _router_kernel(x_ref, wt_ref, b_ref, idx_ref, gate_ref, rank_ref, cnt_ref, carry_ref):
    T = x_ref.shape[0]
    E = N_EXPERTS

    @pl.when(pl.program_id(0) == 0)
    def _():
        carry_ref[...] = jnp.zeros(carry_ref.shape, F32)

    x = x_ref[...]
    wt = wt_ref[...]
    xh = x.astype(BF16)
    xl = (x - xh.astype(F32)).astype(BF16)
    wh = wt.astype(BF16)
    wl = (wt - wh.astype(F32)).astype(BF16)
    logits = _dot_nt(wh, xh) + (_dot_nt(wl, xh) + _dot_nt(wh, xl)) + b_ref[...]

    eidx = lax.broadcasted_iota(jnp.int32, (E, T), 0)
    work = logits
    vals, idxs = [], []
    for _ in range(TOP_K):
        mx = jnp.max(work, axis=0, keepdims=True)
        ix = jnp.min(jnp.where(work == mx, eidx, E), axis=0, keepdims=True)
        vals.append(mx)
        idxs.append(ix)
        work = jnp.where(eidx == ix, -jnp.inf, work)
    exps = [jnp.exp(vv - vals[0]) for vv in vals]
    den = exps[0] + exps[1] + exps[2] + exps[3]

    onehot = jnp.zeros((E, T), F32)
    for ix in idxs:
        onehot = onehot + jnp.where(eidx == ix, 1.0, 0.0)
    ri = lax.broadcasted_iota(jnp.int32, (T, T), 0)
    ci = lax.broadcasted_iota(jnp.int32, (T, T), 1)
    before = jnp.where(ri < ci, 1.0, 0.0).astype(BF16)
    prefix = _dot(onehot.astype(BF16), before) + carry_ref[...]
    carry_ref[...] = carry_ref[...] + jnp.sum(onehot, axis=1, keepdims=True)
    cnt_ref[...] = carry_ref[...]

    for j in range(TOP_K):
        idx_ref[j:j + 1, :] = idxs[j]
        gate_ref[j:j + 1, :] = exps[j] / den
        rk = jnp.sum(jnp.where(eidx == idxs[j], prefix, 0.0), axis=0, keepdims=True)
        rank_ref[j:j + 1, :] = rk.astype(jnp.int32)


def _router(xt, w_router, b_router):
    N, D = xt.shape
    T = MOE_TOK_TILE
    tok = pl.BlockSpec((TOP_K, T), lambda i: (0, i))
    return pl.pallas_call(
        _router_kernel,
        grid=(N // T,),
        in_specs=[
            pl.BlockSpec((T, D), lambda i: (i, 0)),
            pl.BlockSpec((N_EXPERTS, D), lambda i: (0, 0)),
            pl.BlockSpec((N_EXPERTS, 1), lambda i: (0, 0)),
        ],
        out_specs=[tok, tok, tok, pl.BlockSpec((N_EXPERTS, 1), lambda i: (0, 0))],
        out_shape=[
            jax.ShapeDtypeStruct((TOP_K, N), jnp.int32),
            jax.ShapeDtypeStruct((TOP_K, N), F32),
            jax.ShapeDtypeStruct((TOP_K, N), jnp.int32),
            jax.ShapeDtypeStruct((N_EXPERTS, 1), F32),
        ],
        scratch_shapes=[pltpu.VMEM((N_EXPERTS, 1), F32)],
        compiler_params=_cparams(("arbitrary",)),
        name="moe_router",
    )(xt, w_router.T, b_router.reshape(N_EXPERTS, 1))


def _rowmap_kernel(d0_ref, d1_ref, d2_ref, d3_ref, init_ref, map_ref, sem, *, n_tok):
    i = pl.program_id(0)
    toks = d0_ref.shape[0]

    @pl.when(i == 0)
    def _():
        cp = pltpu.make_async_copy(init_ref, map_ref, sem)
        cp.start()
        cp.wait()

    t0 = i * toks

    def body(a, c):
        for j, d_ref in enumerate((d0_ref, d1_ref, d2_ref, d3_ref)):
            map_ref[d_ref[a]] = j * n_tok + t0 + a
        return c

    lax.fori_loop(0, toks, body, 0, unroll=2)


def _rowmap(dest, n_blocks):
    assert dest.shape[0] == TOP_K == 4
    n_tok = dest.shape[1]
    tile = min(n_tok, 2048)
    nt = n_tok // tile
    P = n_blocks * MOE_BLOCK
    init = n_tok * TOP_K + (jnp.arange(P, dtype=jnp.int32) & (MOE_BLOCK - 1))
    dspec = lambda j: pl.BlockSpec((tile,), lambda i: (j * nt + i,), memory_space=pltpu.SMEM)
    dflat = dest.reshape(TOP_K * n_tok)
    return pl.pallas_call(
        functools.partial(_rowmap_kernel, n_tok=n_tok),
        grid=(nt,),
        in_specs=[dspec(0), dspec(1), dspec(2), dspec(3), pl.BlockSpec(memory_space=pl.ANY)],
        out_specs=pl.BlockSpec((P,), lambda i: (0,), memory_space=pltpu.SMEM),
        out_shape=jax.ShapeDtypeStruct((P,), jnp.int32),
        scratch_shapes=[pltpu.SemaphoreType.DMA],
        compiler_params=_cparams(("arbitrary",)),
        name="moe_rowmap",
    )(dflat, dflat, dflat, dflat, init)


def _expert_kernel(blk_e_ref, nused_ref, map_prev_ref, map_cur_ref, map_next_ref,
                   x_ref, w1_ref, b1_ref, w2_ref, b2_ref, y4_ref,
                   xbuf0, xbuf1, ybuf0, ybuf1, act_ref, w1b_ref, w2b_ref, gsem, ssem, *, n_tok):
    i = pl.program_id(0)
    n_used = nused_ref[0]
    BLK = MOE_BLOCK
    spare_row0 = n_tok * TOP_K

    def gather_row(map_ref, r, xdst):
        tok = map_ref[0, r] & (n_tok - 1)
        return pltpu.make_async_copy(x_ref.at[pl.ds(tok, 1)], xdst.at[pl.ds(r, 1)], gsem)

    def scatter_row(dst, r, ysrc):
        return pltpu.make_async_copy(ysrc.at[pl.ds(r, 1)], y4_ref.at[pl.ds(dst, 1)], ssem)

    def wait_rows(sem):
        def body(r, c):
            pltpu.make_async_copy(xbuf0.at[pl.ds(r, 1)], xbuf1.at[pl.ds(r, 1)], sem).wait()
            return c
        lax.fori_loop(0, BLK, body, 0, unroll=8)

    @pl.when(i == 0)
    def _():
        ybuf1[...] = jnp.zeros(ybuf1.shape, F32)

        def first(r, c):
            gather_row(map_cur_ref, r, xbuf0).start()
            return c
        lax.fori_loop(0, BLK, first, 0, unroll=8)

    prev = blk_e_ref[jnp.maximum(i - 1, 0)]
    fresh = (i == 0) | (blk_e_ref[i] != prev)

    @pl.when(fresh & (i < n_used))
    def _():
        w1b_ref[...] = w1_ref[...].astype(BF16)
        w2b_ref[...] = w2_ref[...].astype(BF16)

    def step(xcur, xnext, ycur, yprev):
        wait_rows(gsem)

        @pl.when(i >= 1)
        def _():
            wait_rows(ssem)

        F = w2_ref.shape[0]
        D = w2_ref.shape[1]
        n1, n2 = F // MOE_CHUNK, D // MOE_CHUNK
        rows_per_group = BLK // (n1 + n2)

        def neighbour_dmas(group):
            for r in range(group * rows_per_group, (group + 1) * rows_per_group):
                gather_row(map_next_ref, r, xnext).start()
                dst = jnp.where(i == 0, spare_row0 + r, map_prev_ref[0, r])
                scatter_row(dst, r, yprev).start()

        xb = xcur[...].astype(BF16)
        for c in range(n1):
            gc = slice(c * MOE_CHUNK, (c + 1) * MOE_CHUNK)
            uc = slice(F + c * MOE_CHUNK, F + (c + 1) * MOE_CHUNK)
            g = jnp.minimum(_dot(xb, w1b_ref[:, gc]) + b1_ref[:, gc], SWIGLU_LIMIT)
            u = jnp.clip(_dot(xb, w1b_ref[:, uc]) + b1_ref[:, uc], -SWIGLU_LIMIT, SWIGLU_LIMIT)
            glu = g / (1.0 + jnp.exp(-SWIGLU_ALPHA * g))
            act_ref[:, gc] = ((u + 1.0) * glu).astype(BF16)
            neighbour_dmas(c)
        act = act_ref[...]
        for c in range(n2):
            oc = slice(c * MOE_CHUNK, (c + 1) * MOE_CHUNK)
            ycur[:, oc] = _dot(act, w2b_ref[:, oc]) + b2_ref[:, oc]
            neighbour_dmas(n1 + c)

        @pl.when(i == n_used - 1)
        def _():
            wait_rows(gsem)
            wait_rows(ssem)

            def last(r, c):
                scatter_row(map_cur_ref[0, r], r, ycur).start()
                return c
            lax.fori_loop(0, BLK, last, 0, unroll=8)
            wait_rows(ssem)

    @pl.when((i < n_used) & (i % 2 == 0))
    def _():
        step(xbuf0, xbuf1, ybuf0, ybuf1)

    @pl.when((i < n_used) & (i % 2 == 1))
    def _():
        step(xbuf1, xbuf0, ybuf1, ybuf0)


def _experts(x, row_map, blk_e, n_used, w1, b1, w2, b2, layer):
    N, D = x.shape
    assert N & (N - 1) == 0, "token count must be a power of two (row map decode)"
    nb = row_map.shape[0] // MOE_BLOCK
    _, E, _, F2 = w1.shape
    F = w2.shape[2]
    map3 = row_map.reshape(nb, 1, MOE_BLOCK)
    wspec = lambda r, c: pl.BlockSpec((None, None, r, c), lambda i, be, nu: (layer, be[i], 0, 0))
    mspec = lambda off: pl.BlockSpec(
        (None, 1, MOE_BLOCK), lambda i, be, nu: (jnp.clip(i + off, 0, nb - 1), 0, 0),
        memory_space=pltpu.SMEM)
    grid_spec = pltpu.PrefetchScalarGridSpec(
        num_scalar_prefetch=2,
        grid=(nb,),
        in_specs=[
            mspec(-1), mspec(0), mspec(1),
            pl.BlockSpec(memory_space=pl.ANY),
            wspec(D, F2), wspec(1, F2), wspec(F, D), wspec(1, D),
        ],
        out_specs=pl.BlockSpec(memory_space=pl.ANY),
        scratch_shapes=[pltpu.VMEM((MOE_BLOCK, D), F32), pltpu.VMEM((MOE_BLOCK, D), F32),
                        pltpu.VMEM((MOE_BLOCK, D), F32), pltpu.VMEM((MOE_BLOCK, D), F32),
                        pltpu.VMEM((MOE_BLOCK, F), BF16),
                        pltpu.VMEM((D, F2), BF16), pltpu.VMEM((F, D), BF16),
                        pltpu.SemaphoreType.DMA, pltpu.SemaphoreType.DMA],
    )
    nl = w1.shape[0]
    return pl.pallas_call(
        functools.partial(_expert_kernel, n_tok=N),
        grid_spec=grid_spec,
        out_shape=jax.ShapeDtypeStruct((N * TOP_K + MOE_BLOCK, D), F32),
        compiler_params=_cparams(("arbitrary",)),
        name="moe_experts",
    )(blk_e, n_used, map3, map3, map3, x, w1, b1.reshape(nl, E, 1, F2), w2, b2.reshape(nl, E, 1, D))


def _combine_ln_kernel(gate_ref, x_ref, y0_ref, y1_ref, y2_ref, y3_ref, g_ref, b_ref, o_ref):
    gate = gate_ref[...]
    f = jnp.zeros(x_ref.shape, F32)
    for j, y_ref in enumerate((y0_ref, y1_ref, y2_ref, y3_ref)):
        f = f + y_ref[...] * gate[:, j:j + 1]
    o_ref[...] = _layer_norm(DN_ALPHA * x_ref[...] + f, g_ref[...], b_ref[...])


def _combine_ln(xt, y4, gates, g, b):
    N, D = xt.shape
    T = MOE_TOK_TILE
    nt = N // T
    yspec = lambda j: pl.BlockSpec((T, D), lambda i: (j * nt + i, 0))
    return pl.pallas_call(
        _combine_ln_kernel,
        grid=(nt,),
        in_specs=[
            pl.BlockSpec((T, TOP_K), lambda i: (i, 0)),
            pl.BlockSpec((T, D), lambda i: (i, 0)),
            yspec(0), yspec(1), yspec(2), yspec(3),
            pl.BlockSpec((1, D), lambda i: (0, 0)),
            pl.BlockSpec((1, D), lambda i: (0, 0)),
        ],
        out_specs=pl.BlockSpec((T, D), lambda i: (i, 0)),
        out_shape=jax.ShapeDtypeStruct((N, D), F32),
        compiler_params=_cparams(("parallel",)),
        name="moe_combine_ln",
    )(gates, xt, y4, y4, y4, y4, g.reshape(1, D), b.reshape(1, D))


def _moe_ln(xt, w_router, b_router, w1, b1, w2, b2, g, b, layer):
    N, D = xt.shape
    idx, gates, rank, counts = _router(xt, w_router, b_router)
    counts = counts.reshape(N_EXPERTS).astype(jnp.int32)
    padded = ((counts + MOE_BLOCK - 1) // MOE_BLOCK) * MOE_BLOCK
    pend = jnp.cumsum(padded)
    pstart = pend - padded
    n_blocks = -(-(N * TOP_K) // MOE_BLOCK) + N_EXPERTS
    dest = pstart[idx] + rank
    blk_start = jnp.arange(n_blocks, dtype=jnp.int32) * MOE_BLOCK
    blk_e = jnp.minimum(jnp.sum((pend[None, :] <= blk_start[:, None]).astype(jnp.int32), axis=1),
                        N_EXPERTS - 1)
    n_used = (pend[-1:] // MOE_BLOCK).astype(jnp.int32)
    row_map = _rowmap(dest, n_blocks)
    y4 = _experts(xt, row_map, blk_e, n_used, w1, b1, w2, b2, layer)
    return _combine_ln(xt, y4, gates.T, g, b)


def _dsa_layer(x, w_in, kv_norm, w_uk, w_uv, w_out, g, b):
    Bsz, L, D = x.shape
    qlat, c, ct, qi, ki, wi = _dsa_proj(x, w_in, kv_norm, w_uk)
    o = _dsa_attention(qi, wi, ki, qlat, ct, c, w_uv)
    return _outproj_ln(o.reshape(Bsz * L, -1), w_out, x.reshape(Bsz * L, D), g, b)


def _gla_layer(x, w_in, w_g2, g_bias, norm_g, w_out, g, b):
    Bsz, L, D = x.shape
    xt = x.reshape(Bsz * L, D)
    q, k, v, la, r = _gla_proj(xt, w_in, w_g2, g_bias)
    sh = lambda t: t.reshape(Bsz, L, t.shape[-1])
    o = _gla_core(sh(q), sh(k), sh(v), sh(la), sh(r), norm_g)
    return _outproj_ln(o.reshape(Bsz * L, -1), w_out, xt, g, b)


def kernel(x, a_w_in, a_kv_norm, a_w_uk, a_w_uv, a_w_out, b_w_in, b_w_g2, b_g_bias, b_norm, b_w_out, m_w_router, m_b_router, m_w1, m_b1, m_w2, m_b2, ln1_g, ln1_b, ln2_g, ln2_b):
    Bsz, L, D = x.shape
    for i in range(DEPTH):
        j = i // 2
        if i % 2 == 0:
            xt = _dsa_layer(x, a_w_in[j], a_kv_norm[j], a_w_uk[j], a_w_uv[j], a_w_out[j],
                            ln1_g[i], ln1_b[i])
        else:
            xt = _gla_layer(x, b_w_in[j], b_w_g2[j], b_g_bias[j], b_norm[j], b_w_out[j],
                            ln1_g[i], ln1_b[i])
        xt = _moe_ln(xt, m_w_router[i], m_b_router[i], m_w1, m_b1, m_w2, m_b2,
                     ln2_g[i], ln2_b[i], i)
        x = xt.reshape(Bsz, L, D)
    return x
```

```python
import functools

import jax
import jax.numpy as jnp
from jax import lax
from jax.experimental import pallas as pl
from jax.experimental.pallas import tpu as pltpu

F32 = jnp.float32
BF16 = jnp.bfloat16

DEPTH = 2
DN_ALPHA = (2.0 * DEPTH) ** 0.25
LN_EPS = 1e-5
RMS_EPS = 1e-6

A_HEADS = 16
A_QK_DIM = 64
A_V_DIM = 64
A_LATENT = 256
IDX_HEADS = 8
IDX_DIM = 64
IDX_TOPK_MAX = 256
A_QBLK = 128
A_KBLK = 512
A_ONES = 16
CNT_ROWS = 32
BIS_STEPS = 4
BIS_MAX_STEPS = 36
BIS_FIRST_PIVOT = 0.0625
LOG2E = 1.4426950408889634

B_HEADS = 4
B_GATE_RANK = 16
B_GATE_TAU = 16.0
B_CHUNK = 64
B_SUB = 16
B_TBLK = 512

N_EXPERTS = 32
TOP_K = 4
SWIGLU_LIMIT = 7.0
SWIGLU_ALPHA = 1.702
MOE_BLOCK = 256
MOE_CHUNK = 256
ROW_TILE = 512
MOE_TOK_TILE = 256

VMEM_LIMIT = 56 * 1024 * 1024
NEG_BIG = -1e30


def _cparams(sem):
    return pltpu.CompilerParams(dimension_semantics=sem, vmem_limit_bytes=VMEM_LIMIT)


def _dot(a, b):
    return jnp.dot(a, b, preferred_element_type=F32)


def _dot_nt(a, b):
    return lax.dot_general(a, b, (((1,), (1,)), ((), ())), preferred_element_type=F32)


def _dot_tn(a, b):
    return lax.dot_general(a, b, (((0,), (0,)), ((), ())), preferred_element_type=F32)


def _layer_norm(z, g, b):
    mu = jnp.mean(z, axis=-1, keepdims=True)
    zc = z - mu
    var = jnp.mean(zc * zc, axis=-1, keepdims=True)
    return zc * lax.rsqrt(var + LN_EPS) * g + b


def _dsa_proj_kernel(x_ref, wqt_ref, wc_ref, wqit_ref, wki_ref, wwit_ref, kvn_ref, wukt_ref,
                     qlat_ref, c_ref, ct_ref, qi_ref, ki_ref, wi_ref):
    T = x_ref.shape[0]
    QB = A_QBLK
    xb = x_ref[...].astype(BF16)
    qt = _dot_nt(wqt_ref[...], xb)
    for h in range(A_HEADS):
        qh = qt[h * A_QK_DIM:(h + 1) * A_QK_DIM, :].astype(BF16)
        ql = (_dot(wukt_ref[h], qh) * (A_QK_DIM ** -0.5 * LOG2E)).astype(BF16)
        for j in range(T // QB):
            qlat_ref[j, :, h * QB:(h + 1) * QB] = ql[:, j * QB:(j + 1) * QB]
    c = _dot(xb, wc_ref[...])
    c = c * lax.rsqrt(jnp.mean(c * c, axis=-1, keepdims=True) + RMS_EPS) * kvn_ref[...]
    c_ref[...] = c.astype(BF16)
    ct_ref[:A_LATENT, :] = c.T.astype(BF16)
    ct_ref[A_LATENT:, :] = jnp.ones((A_ONES, T), BF16)
    qit = _dot_nt(wqit_ref[...], xb) * (IDX_DIM ** -0.5)
    for h in range(IDX_HEADS):
        qh = qit[h * IDX_DIM:(h + 1) * IDX_DIM, :].astype(BF16)
        for j in range(T // QB):
            qi_ref[j, :, h * QB:(h + 1) * QB] = qh[:, j * QB:(j + 1) * QB]
    ki_ref[...] = _dot(xb, wki_ref[...]).astype(BF16)
    wi_ref[...] = _dot_nt(wwit_ref[...], xb) * (IDX_HEADS ** -0.5)


def _dsa_proj(x, w_in, kv_norm, w_uk):
    Bsz, L, D = x.shape
    T, QB = A_KBLK, A_QBLK
    nkb, nqb, qpt = L // T, L // QB, T // QB
    o1 = A_HEADS * A_QK_DIM
    o2 = o1 + A_LATENT
    o3 = o2 + IDX_HEADS * IDX_DIM
    o4 = o3 + IDX_DIM
    wb = w_in.astype(BF16)
    wqt, wc, wqit, wki, wwit = wb[:, :o1].T, wb[:, o1:o2], wb[:, o2:o3].T, wb[:, o3:o4], wb[:, o4:].T
    wukt = jnp.swapaxes(w_uk, 1, 2).astype(BF16)
    full = lambda shape: pl.BlockSpec(shape, lambda b, t: (0,) * len(shape))
    return pl.pallas_call(
        _dsa_proj_kernel,
        grid=(Bsz, nkb),
        in_specs=[
            pl.BlockSpec((None, T, D), lambda b, t: (b, t, 0)),
            full(wqt.shape), full(wc.shape), full(wqit.shape), full(wki.shape), full(wwit.shape),
            full((1, A_LATENT)), full(wukt.shape),
        ],
        out_specs=[
            pl.BlockSpec((None, qpt, A_LATENT, A_HEADS * QB), lambda b, t: (b, t, 0, 0)),
            pl.BlockSpec((None, None, T, A_LATENT), lambda b, t: (b, t, 0, 0)),
            pl.BlockSpec((None, None, A_LATENT + A_ONES, T), lambda b, t: (b, t, 0, 0)),
            pl.BlockSpec((None, qpt, IDX_DIM, IDX_HEADS * QB), lambda b, t: (b, t, 0, 0)),
            pl.BlockSpec((None, None, T, IDX_DIM), lambda b, t: (b, t, 0, 0)),
            pl.BlockSpec((None, IDX_HEADS, T), lambda b, t: (b, 0, t)),
        ],
        out_shape=[
            jax.ShapeDtypeStruct((Bsz, nqb, A_LATENT, A_HEADS * QB), BF16),
            jax.ShapeDtypeStruct((Bsz, nkb, T, A_LATENT), BF16),
            jax.ShapeDtypeStruct((Bsz, nkb, A_LATENT + A_ONES, T), BF16),
            jax.ShapeDtypeStruct((Bsz, nqb, IDX_DIM, IDX_HEADS * QB), BF16),
            jax.ShapeDtypeStruct((Bsz, nkb, T, IDX_DIM), BF16),
            jax.ShapeDtypeStruct((Bsz, IDX_HEADS, L), F32),
        ],
        compiler_params=_cparams(("parallel", "parallel")),
        name="dsa_proj",
    )(x, wqt, wc, wqit, wki, wwit, kv_norm.reshape(1, A_LATENT), wukt)


def _ordered_int_to_float(o):
    bits = jnp.where(o >= 0, o, o ^ jnp.int32(2 ** 31 - 1))
    return lax.bitcast_convert_type(bits, F32)


def _float_to_ordered_int(x):
    bits = lax.bitcast_convert_type(jnp.asarray(x, F32), jnp.int32)
    return jnp.where(bits >= 0, bits, bits ^ jnp.int32(2 ** 31 - 1))


def _dsa_attn_kernel(qi_ref, wi_ref, ki_ref, qlat_ref, ct_ref, c_ref, wuvt_ref, o_ref,
                     sc_ref, s_ref, m_ref, acc_ref, ot_ref, jcut_ref, *, topk, seq_len):
    QB, KB = A_QBLK, A_KBLK
    qb = pl.program_id(1)
    q0 = qb * QB
    nvb = (q0 + QB + KB - 1) // KB
    qpos = q0 + lax.broadcasted_iota(jnp.int32, (1, QB), 1)
    krow = lax.broadcasted_iota(jnp.int32, (KB, QB), 0)
    kf = float(topk)

    wi = wi_ref[...]
    qi = qi_ref[...]

    def score_body(kb, carry):
        s = _dot(ki_ref[kb], qi)
        sc = jnp.zeros((KB, QB), F32)
        for h in range(IDX_HEADS):
            sc = sc + jnp.maximum(s[:, h * QB:(h + 1) * QB], 0.0) * wi[h:h + 1, :]
        sc_ref[kb] = jnp.where((kb * KB + krow) <= qpos, sc, -jnp.inf)
        return carry

    lax.fori_loop(0, nvb, score_body, 0)

    def count(pred):
        def body(kb, acc):
            hit = jnp.where(pred(sc_ref[kb], kb), 1.0, 0.0)
            return acc + jnp.sum(hit.reshape(KB // CNT_ROWS, CNT_ROWS, QB), axis=0)
        acc = lax.fori_loop(0, nvb, body, jnp.zeros((CNT_ROWS, QB), F32))
        return jnp.sum(acc, axis=0, keepdims=True)

    def row_max():
        def body(kb, acc):
            return jnp.maximum(acc, jnp.max(sc_ref[kb].reshape(KB // CNT_ROWS, CNT_ROWS, QB), axis=0))
        acc = lax.fori_loop(0, nvb, body, jnp.full((CNT_ROWS, QB), -jnp.inf, F32))
        return jnp.max(acc, axis=0, keepdims=True)

    def half_gap(lo, hi):
        return lax.shift_right_logical(hi - lo, jnp.int32(1))

    def bis_step(st, mid):
        lo, hi, thr, done = st
        cand = _ordered_int_to_float(mid)
        cnt = count(lambda blk, kb: blk >= cand)
        ge = cnt >= kf
        hit = (cnt == kf) & (done < 0.5)
        return (jnp.where(ge, mid, lo), jnp.where(ge, hi, mid),
                jnp.where(hit, cand, thr), jnp.where(hit, 1.0, done))

    def bis_cond(st):
        i, lo, hi, _, done = st
        open_ = jnp.where((done < 0.5) & (half_gap(lo, hi) > 0), 1.0, 0.0)
        return (i < BIS_MAX_STEPS) & (jnp.max(open_) > 0.5)

    def bis_body(st):
        i, st = st[0], st[1:]
        for _ in range(BIS_STEPS):
            st = bis_step(st, st[0] + half_gap(st[0], st[1]))
        return (i + BIS_STEPS,) + st

    smax = row_max()
    lo0 = jnp.full((1, QB), _float_to_ordered_int(-jnp.inf), jnp.int32)
    hi0 = _float_to_ordered_int(smax) + 1
    done0 = jnp.where(qpos < topk, 1.0, 0.0)
    st = (lo0, hi0, jnp.full((1, QB), -jnp.inf, F32), done0)
    first = jnp.where(smax > 0.0, _float_to_ordered_int(smax * BIS_FIRST_PIVOT),
                      lo0 + half_gap(lo0, hi0))
    st = bis_step(st, first)
    _, u, _, thr, done = lax.while_loop(bis_cond, bis_body, (jnp.int32(0),) + st)
    thr = jnp.where(done > 0.5, thr, _ordered_int_to_float(u))
    jcut_ref[...] = jnp.full((1, QB), seq_len, jnp.int32)

    @pl.when(jnp.min(done) < 0.5)
    def _():
        need = kf - count(lambda blk, kb: blk > thr)
        nbits = max(1, (seq_len - 1).bit_length())

        def jbody(i, x):
            trial = x | jnp.left_shift(jnp.int32(1), nbits - 1 - i)
            g = count(lambda blk, kb: (blk == thr) & ((kb * KB + krow) < trial))
            return jnp.where(g < need, trial, x)

        x = lax.fori_loop(0, nbits, jbody, jnp.zeros((1, QB), jnp.int32))
        jcut_ref[...] = jnp.where(done > 0.5, seq_len, x)

    jcut = jcut_ref[...]

    m_ref[...] = jnp.full(m_ref.shape, NEG_BIG, F32)
    acc_ref[...] = jnp.zeros(acc_ref.shape, F32)

    pair_cols = [slice(hp * 2 * QB, (hp + 1) * 2 * QB) for hp in range(A_HEADS // 2)]

    def logits(kb, cols):
        return _dot(c_ref[kb], qlat_ref[:, cols])

    def mask_bias(kb):
        blk = sc_ref[kb]
        kpos = kb * KB + krow
        sel = ((blk > thr) | ((blk == thr) & (kpos <= jcut))) & (kpos <= qpos)
        bias = jnp.where(sel, 0.0, NEG_BIG)
        return jnp.concatenate([bias, bias], axis=1)

    def consume(kb, s, cols, bias2):
        s = s + bias2
        m_prev = m_ref[:, cols]
        m_new = jnp.maximum(m_prev, jnp.max(s, axis=0, keepdims=True))
        alpha = jnp.exp2(m_prev - m_new)
        p = jnp.exp2(s - m_new).astype(BF16)
        acc_ref[:, cols] = alpha * acc_ref[:, cols] + _dot(ct_ref[kb], p)
        m_ref[:, cols] = m_new

    for cols in pair_cols:
        s_ref[:, cols] = logits(0, cols)

    def attn_body(kb, carry):
        bias2 = mask_bias(kb)
        for cols in pair_cols:
            s = s_ref[:, cols]
            s_ref[:, cols] = logits(kb + 1, cols)
            consume(kb, s, cols, bias2)
        return carry

    lax.fori_loop(0, nvb - 1, attn_body, 0)
    bias2 = mask_bias(nvb - 1)
    for cols in pair_cols:
        consume(nvb - 1, s_ref[:, cols], cols, bias2)

    for h in range(A_HEADS):
        cols = slice(h * QB, (h + 1) * QB)
        oh = acc_ref[:A_LATENT, cols] / acc_ref[A_LATENT:A_LATENT + 1, cols]
        ot_ref[h * A_V_DIM:(h + 1) * A_V_DIM, :] = _dot(wuvt_ref[h], oh.astype(BF16))
    o_ref[...] = ot_ref[...].T.astype(BF16)


def _dsa_attention(qi, wi, ki, qlat, ct, c, w_uv):
    Bsz, nqb, _, _ = qlat.shape
    QB, KB = A_QBLK, A_KBLK
    L = nqb * QB
    nkb = L // KB
    topk = min(IDX_TOPK_MAX, L // 4)
    wuvt = jnp.swapaxes(w_uv, 1, 2).astype(BF16)
    kern = functools.partial(_dsa_attn_kernel, topk=topk, seq_len=L)
    return pl.pallas_call(
        kern,
        grid=(Bsz, nqb),
        in_specs=[
            pl.BlockSpec((None, None, IDX_DIM, IDX_HEADS * QB), lambda b, q: (b, q, 0, 0)),
            pl.BlockSpec((None, IDX_HEADS, QB), lambda b, q: (b, 0, q)),
            pl.BlockSpec((None, nkb, KB, IDX_DIM), lambda b, q: (b, 0, 0, 0)),
            pl.BlockSpec((None, None, A_LATENT, A_HEADS * QB), lambda b, q: (b, q, 0, 0)),
            pl.BlockSpec((None, nkb, A_LATENT + A_ONES, KB), lambda b, q: (b, 0, 0, 0)),
            pl.BlockSpec((None, nkb, KB, A_LATENT), lambda b, q: (b, 0, 0, 0)),
            pl.BlockSpec((A_HEADS, A_V_DIM, A_LATENT), lambda b, q: (0, 0, 0)),
        ],
        out_specs=pl.BlockSpec((None, QB, A_HEADS * A_V_DIM), lambda b, q: (b, q, 0)),
        out_shape=jax.ShapeDtypeStruct((Bsz, L, A_HEADS * A_V_DIM), BF16),
        scratch_shapes=[
            pltpu.VMEM((nkb, KB, QB), F32),
            pltpu.VMEM((KB, A_HEADS * QB), F32),
            pltpu.VMEM((1, A_HEADS * QB), F32),
            pltpu.VMEM((A_LATENT + A_ONES, A_HEADS * QB), F32),
            pltpu.VMEM((A_HEADS * A_V_DIM, QB), F32),
            pltpu.VMEM((1, QB), jnp.int32),
        ],
        compiler_params=_cparams(("parallel", "parallel")),
        name="dsa_attn",
    )(qi, wi, ki, qlat, ct, c, wuvt)


def _outproj_ln_kernel(o_ref, w_ref, x_ref, g_ref, b_ref, y_ref):
    h = _dot(o_ref[...], w_ref[...])
    y_ref[...] = _layer_norm(DN_ALPHA * x_ref[...] + h, g_ref[...], b_ref[...])


def _outproj_ln(o, w_out, x, g, b):
    N, K = o.shape
    D = x.shape[1]
    T = ROW_TILE
    return pl.pallas_call(
        _outproj_ln_kernel,
        grid=(N // T,),
        in_specs=[
            pl.BlockSpec((T, K), lambda i: (i, 0)),
            pl.BlockSpec((K, D), lambda i: (0, 0)),
            pl.BlockSpec((T, D), lambda i: (i, 0)),
            pl.BlockSpec((1, D), lambda i: (0, 0)),
            pl.BlockSpec((1, D), lambda i: (0, 0)),
        ],
        out_specs=pl.BlockSpec((T, D), lambda i: (i, 0)),
        out_shape=jax.ShapeDtypeStruct((N, D), F32),
        compiler_params=_cparams(("parallel",)),
        name="outproj_ln",
    )(o, w_out.astype(BF16), x, g.reshape(1, D), b.reshape(1, D))


def _gla_proj_kernel(x_ref, wq_ref, wk_ref, wv_ref, wg_ref, wr_ref, wg2_ref, gb_ref,
                     q_ref, k_ref, v_ref, la_ref, r_ref, *, kh):
    xb = x_ref[...].astype(BF16)
    q_ref[...] = _dot(xb, wq_ref[...]) * (kh ** -0.5)
    k_ref[...] = _dot(xb, wk_ref[...])
    v_ref[...] = _dot(xb, wv_ref[...]).astype(BF16)
    r_ref[...] = _dot(xb, wr_ref[...])
    glr = _dot(xb, wg_ref[...])
    z = _dot(glr.astype(BF16), wg2_ref[...]) + gb_ref[...]
    log_sig = jnp.minimum(z, 0.0) - jnp.log(1.0 + jnp.exp(-jnp.abs(z)))
    la_ref[...] = log_sig / B_GATE_TAU


def _gla_proj(x, w_in, w_g2, g_bias):
    N, D = x.shape
    DK = w_g2.shape[1]
    DV = D
    T = ROW_TILE
    wb = w_in.astype(BF16)
    o1, o2, o3, o4 = DK, 2 * DK, 2 * DK + DV, 2 * DK + DV + B_GATE_RANK
    wq, wk, wv, wg, wr = wb[:, :o1], wb[:, o1:o2], wb[:, o2:o3], wb[:, o3:o4], wb[:, o4:]
    full = lambda shape: pl.BlockSpec(shape, lambda i: (0,) * len(shape))
    row = lambda w: pl.BlockSpec((T, w), lambda i: (i, 0))
    kern = functools.partial(_gla_proj_kernel, kh=DK // B_HEADS)
    return pl.pallas_call(
        kern,
        grid=(N // T,),
        in_specs=[row(D), full(wq.shape), full(wk.shape), full(wv.shape), full(wg.shape),
                  full(wr.shape), full(w_g2.shape), full((1, DK))],
        out_specs=[row(DK), row(DK), row(DV), row(DK), row(DV)],
        out_shape=[
            jax.ShapeDtypeStruct((N, DK), F32),
            jax.ShapeDtypeStruct((N, DK), F32),
            jax.ShapeDtypeStruct((N, DV), BF16),
            jax.ShapeDtypeStruct((N, DK), F32),
            jax.ShapeDtypeStruct((N, DV), F32),
        ],
        compiler_params=_cparams(("parallel",)),
        name="gla_proj",
    )(x, wq, wk, wv, wg, wr, w_g2.astype(BF16), g_bias.reshape(1, DK))


def _split3(a):
    hi = a.astype(BF16)
    r1 = a - hi.astype(F32)
    mid = r1.astype(BF16)
    lo = (r1 - mid.astype(F32)).astype(BF16)
    return hi, mid, lo


def _gla_kernel(q_ref, k_ref, v_ref, la_ref, r_ref, ng_ref, o_ref, st_ref, a_ref):
    C, SB = B_CHUNK, B_SUB
    nsub = C // SB
    kh = q_ref.shape[1] // B_HEADS
    vh = v_ref.shape[1] // B_HEADS

    @pl.when(pl.program_id(1) == 0)
    def _():
        st_ref[...] = jnp.zeros(st_ref.shape, F32)

    ri = lax.broadcasted_iota(jnp.int32, (C, C), 0)
    ci = lax.broadcasted_iota(jnp.int32, (C, C), 1)
    tril = jnp.where(ci <= ri, 1.0, 0.0).astype(BF16)
    sub_r = lax.broadcasted_iota(jnp.int32, (SB, 1), 0)
    sub_c = lax.broadcasted_iota(jnp.int32, (1, SB), 1)

    def head_chunk(rows, h):
        kc = slice(h * kh, (h + 1) * kh)
        vc = slice(h * vh, (h + 1) * vh)
        q = q_ref[rows, kc]
        k = k_ref[rows, kc]
        v = v_ref[rows, vc]
        la = la_ref[rows, kc]
        hi, mid, lo = _split3(la)
        b = _dot(tril, hi) + _dot(tril, mid) + _dot(tril, lo)
        st = st_ref[h]
        o = _dot_nt((q * jnp.exp(b)).astype(BF16), st.astype(BF16))

        a_ref[h] = jnp.zeros((C, C), F32)
        for I in range(nsub):
            r0 = I * SB
            bI = b[r0:r0 + SB, :]
            qI = q[r0:r0 + SB, :]
            kI = k[r0:r0 + SB, :]
            if I > 0:
                ref_lvl = b[r0 - 1:r0, :]
                qs = (qI * jnp.exp(bI - ref_lvl)).astype(BF16)
                ks = (k[:r0, :] * jnp.exp(ref_lvl - b[:r0, :])).astype(BF16)
                a_ref[h, r0:r0 + SB, :r0] = _dot_nt(qs, ks)
            diag = jnp.zeros((SB, SB), F32)
            for j in range(SB):
                dlt = jnp.where(sub_r >= j, bI - bI[j:j + 1, :], -jnp.inf)
                col = jnp.sum(qI * kI[j:j + 1, :] * jnp.exp(dlt), axis=1, keepdims=True)
                diag = diag + col * jnp.where(sub_c == j, 1.0, 0.0)
            a_ref[h, r0:r0 + SB, r0:r0 + SB] = diag
        o = o + _dot(a_ref[h].astype(BF16), v)

        b_last = b[C - 1:C, :]
        kd = (k * jnp.exp(b_last - b)).astype(BF16)
        st_ref[h] = st * jnp.exp(b_last) + _dot_tn(v, kd)

        o = o * lax.rsqrt(jnp.mean(o * o, axis=-1, keepdims=True) + RMS_EPS) * ng_ref[:, vc]
        r = r_ref[rows, vc]
        o_ref[rows, vc] = (o * (r / (1.0 + jnp.exp(-r)))).astype(BF16)

    def chunk_body(ch, carry):
        rows = pl.ds(pl.multiple_of(ch * C, C), C)
        for h in range(B_HEADS):
            head_chunk(rows, h)
        return carry

    lax.fori_loop(0, q_ref.shape[0] // C, chunk_body, 0)


def _gla_core(q, k, v, la, r, norm_g):
    Bsz, L, DK = q.shape
    DV = v.shape[2]
    kh, vh = DK // B_HEADS, DV // B_HEADS
    T = B_TBLK
    kspec = pl.BlockSpec((None, T, DK), lambda b, t: (b, t, 0))
    vspec = pl.BlockSpec((None, T, DV), lambda b, t: (b, t, 0))
    return pl.pallas_call(
        _gla_kernel,
        grid=(Bsz, L // T),
        in_specs=[kspec, kspec, vspec, kspec, vspec, pl.BlockSpec((1, DV), lambda b, t: (0, 0))],
        out_specs=vspec,
        out_shape=jax.ShapeDtypeStruct((Bsz, L, DV), BF16),
        scratch_shapes=[pltpu.VMEM((B_HEADS, vh, kh), F32),
                        pltpu.VMEM((B_HEADS, B_CHUNK, B_CHUNK), F32)],
        compiler_params=_cparams(("parallel", "arbitrary")),
        name="gla_core",
    )(q, k, v, la, r, norm_g.reshape(1, DV))


def _router_kernel(x_ref, wt_ref, b_ref, idx_ref, gate_ref, rank_ref, cnt_ref, carry_ref):
    T = x_ref.shape[0]
    E = N_EXPERTS

    @pl.when(pl.program_id(0) == 0)
    def _():
        carry_ref[...] = jnp.zeros(carry_ref.shape, F32)

    x = x_ref[...]
    wt = wt_ref[...]
    xh = x.astype(BF16)
    xl = (x - xh.astype(F32)).astype(BF16)
    wh = wt.astype(BF16)
    wl = (wt - wh.astype(F32)).astype(BF16)
    logits = _dot_nt(wh, xh) + (_dot_nt(wl, xh) + _dot_nt(wh, xl)) + b_ref[...]

    eidx = lax.broadcasted_iota(jnp.int32, (E, T), 0)
    work = logits
    vals, idxs = [], []
    for _ in range(TOP_K):
        mx = jnp.max(work, axis=0, keepdims=True)
        ix = jnp.min(jnp.where(work == mx, eidx, E), axis=0, keepdims=True)
        vals.append(mx)
        idxs.append(ix)
        work = jnp.where(eidx == ix, -jnp.inf, work)
    exps = [jnp.exp(vv - vals[0]) for vv in vals]
    den = exps[0] + exps[1] + exps[2] + exps[3]

    onehot = jnp.zeros((E, T), F32)
    for ix in idxs:
        onehot = onehot + jnp.where(eidx == ix, 1.0, 0.0)
    ri = lax.broadcasted_iota(jnp.int32, (T, T), 0)
    ci = lax.broadcasted_iota(jnp.int32, (T, T), 1)
    before = jnp.where(ri < ci, 1.0, 0.0).astype(BF16)
    prefix = _dot(onehot.astype(BF16), before) + carry_ref[...]
    carry_ref[...] = carry_ref[...] + jnp.sum(onehot, axis=1, keepdims=True)
    cnt_ref[...] = carry_ref[...]

    for j in range(TOP_K):
        idx_ref[j:j + 1, :] = idxs[j]
        gate_ref[j:j + 1, :] = exps[j] / den
        rk = jnp.sum(jnp.where(eidx == idxs[j], prefix, 0.0), axis=0, keepdims=True)
        rank_ref[j:j + 1, :] = rk.astype(jnp.int32)


def _router(xt, w_router, b_router):
    N, D = xt.shape
    T = MOE_TOK_TILE
    tok = pl.BlockSpec((TOP_K, T), lambda i: (0, i))
    return pl.pallas_call(
        _router_kernel,
        grid=(N // T,),
        in_specs=[
            pl.BlockSpec((T, D), lambda i: (i, 0)),
            pl.BlockSpec((N_EXPERTS, D), lambda i: (0, 0)),
            pl.BlockSpec((N_EXPERTS, 1), lambda i: (0, 0)),
        ],
        out_specs=[tok, tok, tok, pl.BlockSpec((N_EXPERTS, 1), lambda i: (0, 0))],
        out_shape=[
            jax.ShapeDtypeStruct((TOP_K, N), jnp.int32),
            jax.ShapeDtypeStruct((TOP_K, N), F32),
            jax.ShapeDtypeStruct((TOP_K, N), jnp.int32),
            jax.ShapeDtypeStruct((N_EXPERTS, 1), F32),
        ],
        scratch_shapes=[pltpu.VMEM((N_EXPERTS, 1), F32)],
        compiler_params=_cparams(("arbitrary",)),
        name="moe_router",
    )(xt, w_router.T, b_router.reshape(N_EXPERTS, 1))


def _rowmap_kernel(dest_ref, init_ref, map_ref, sem, *, n_tok):
    i = pl.program_id(0)
    toks = dest_ref.shape[0] // TOP_K

    @pl.when(i == 0)
    def _():
        cp = pltpu.make_async_copy(init_ref, map_ref, sem)
        cp.start()
        cp.wait()

    t0 = i * toks

    def body(a, c):
        for j in range(TOP_K):
            map_ref[dest_ref[a * TOP_K + j]] = j * n_tok + t0 + a
        return c

    lax.fori_loop(0, toks, body, 0, unroll=2)


def _rowmap(dest_flat, n_tok, n_blocks):
    NK = dest_flat.shape[0]
    tile = min(NK, 8192)
    P = n_blocks * MOE_BLOCK
    init = n_tok * TOP_K + (jnp.arange(P, dtype=jnp.int32) & (MOE_BLOCK - 1))
    return pl.pallas_call(
        functools.partial(_rowmap_kernel, n_tok=n_tok),
        grid=(NK // tile,),
        in_specs=[pl.BlockSpec((tile,), lambda i: (i,), memory_space=pltpu.SMEM),
                  pl.BlockSpec(memory_space=pl.ANY)],
        out_specs=pl.BlockSpec((P,), lambda i: (0,), memory_space=pltpu.SMEM),
        out_shape=jax.ShapeDtypeStruct((P,), jnp.int32),
        scratch_shapes=[pltpu.SemaphoreType.DMA],
        compiler_params=_cparams(("arbitrary",)),
        name="moe_rowmap",
    )(dest_flat, init)


def _expert_kernel(blk_e_ref, nused_ref, map_prev_ref, map_cur_ref, map_next_ref,
                   x_ref, w1_ref, b1_ref, w2_ref, b2_ref, y4_ref,
                   xbuf0, xbuf1, ybuf0, ybuf1, act_ref, w1b_ref, w2b_ref, gsem, ssem, *, n_tok):
    i = pl.program_id(0)
    n_used = nused_ref[0]
    BLK = MOE_BLOCK
    spare_row0 = n_tok * TOP_K

    def gather_row(map_ref, r, xdst):
        tok = map_ref[0, r] & (n_tok - 1)
        return pltpu.make_async_copy(x_ref.at[pl.ds(tok, 1)], xdst.at[pl.ds(r, 1)], gsem)

    def scatter_row(dst, r, ysrc):
        return pltpu.make_async_copy(ysrc.at[pl.ds(r, 1)], y4_ref.at[pl.ds(dst, 1)], ssem)

    def wait_rows(sem):
        pltpu.make_async_copy(xbuf0, xbuf1, sem).wait()

    @pl.when(i == 0)
    def _():
        ybuf1[...] = jnp.zeros(ybuf1.shape, F32)

        def first(r, c):
            gather_row(map_cur_ref, r, xbuf0).start()
            return c
        lax.fori_loop(0, BLK, first, 0, unroll=8)

    prev = blk_e_ref[jnp.maximum(i - 1, 0)]
    fresh = (i == 0) | (blk_e_ref[i] != prev)

    @pl.when(fresh & (i < n_used))
    def _():
        w1b_ref[...] = w1_ref[...].astype(BF16)
        w2b_ref[...] = w2_ref[...].astype(BF16)

    def step(xcur, xnext, ycur, yprev):
        wait_rows(gsem)

        @pl.when(i >= 1)
        def _():
            wait_rows(ssem)

        F = w2_ref.shape[0]
        D = w2_ref.shape[1]
        n1, n2 = F // MOE_CHUNK, D // MOE_CHUNK
        rows_per_group = BLK // (n1 + n2)

        def neighbour_dmas(group):
            for r in range(group * rows_per_group, (group + 1) * rows_per_group):
                gather_row(map_next_ref, r, xnext).start()
                dst = jnp.where(i == 0, spare_row0 + r, map_prev_ref[0, r])
                scatter_row(dst, r, yprev).start()

        xb = xcur[...].astype(BF16)
        for c in range(n1):
            gc = slice(c * MOE_CHUNK, (c + 1) * MOE_CHUNK)
            uc = slice(F + c * MOE_CHUNK, F + (c + 1) * MOE_CHUNK)
            g = jnp.minimum(_dot(xb, w1b_ref[:, gc]) + b1_ref[:, gc], SWIGLU_LIMIT)
            u = jnp.clip(_dot(xb, w1b_ref[:, uc]) + b1_ref[:, uc], -SWIGLU_LIMIT, SWIGLU_LIMIT)
            glu = g / (1.0 + jnp.exp(-SWIGLU_ALPHA * g))
            act_ref[:, gc] = ((u + 1.0) * glu).astype(BF16)
            neighbour_dmas(c)
        act = act_ref[...]
        for c in range(n2):
            oc = slice(c * MOE_CHUNK, (c + 1) * MOE_CHUNK)
            ycur[:, oc] = _dot(act, w2b_ref[:, oc]) + b2_ref[:, oc]
            neighbour_dmas(n1 + c)

        @pl.when(i == n_used - 1)
        def _():
            wait_rows(gsem)
            wait_rows(ssem)

            def last(r, c):
                scatter_row(map_cur_ref[0, r], r, ycur).start()
                return c
            lax.fori_loop(0, BLK, last, 0, unroll=8)
            wait_rows(ssem)

    @pl.when((i < n_used) & (i % 2 == 0))
    def _():
        step(xbuf0, xbuf1, ybuf0, ybuf1)

    @pl.when((i < n_used) & (i % 2 == 1))
    def _():
        step(xbuf1, xbuf0, ybuf1, ybuf0)


def _experts(x, row_map, blk_e, n_used, w1, b1, w2, b2, layer):
    N, D = x.shape
    assert N & (N - 1) == 0, "token count must be a power of two (row map decode)"
    nb = row_map.shape[0] // MOE_BLOCK
    _, E, _, F2 = w1.shape
    F = w2.shape[2]
    map3 = row_map.reshape(nb, 1, MOE_BLOCK)
    wspec = lambda r, c: pl.BlockSpec((None, None, r, c), lambda i, be, nu: (layer, be[i], 0, 0))
    mspec = lambda off: pl.BlockSpec(
        (None, 1, MOE_BLOCK), lambda i, be, nu: (jnp.clip(i + off, 0, nb - 1), 0, 0),
        memory_space=pltpu.SMEM)
    grid_spec = pltpu.PrefetchScalarGridSpec(
        num_scalar_prefetch=2,
        grid=(nb,),
        in_specs=[
            mspec(-1), mspec(0), mspec(1),
            pl.BlockSpec(memory_space=pl.ANY),
            wspec(D, F2), wspec(1, F2), wspec(F, D), wspec(1, D),
        ],
        out_specs=pl.BlockSpec(memory_space=pl.ANY),
        scratch_shapes=[pltpu.VMEM((MOE_BLOCK, D), F32), pltpu.VMEM((MOE_BLOCK, D), F32),
                        pltpu.VMEM((MOE_BLOCK, D), F32), pltpu.VMEM((MOE_BLOCK, D), F32),
                        pltpu.VMEM((MOE_BLOCK, F), BF16),
                        pltpu.VMEM((D, F2), BF16), pltpu.VMEM((F, D), BF16),
                        pltpu.SemaphoreType.DMA, pltpu.SemaphoreType.DMA],
    )
    nl = w1.shape[0]
    return pl.pallas_call(
        functools.partial(_expert_kernel, n_tok=N),
        grid_spec=grid_spec,
        out_shape=jax.ShapeDtypeStruct((N * TOP_K + MOE_BLOCK, D), F32),
        compiler_params=_cparams(("arbitrary",)),
        name="moe_experts",
    )(blk_e, n_used, map3, map3, map3, x, w1, b1.reshape(nl, E, 1, F2), w2, b2.reshape(nl, E, 1, D))


def _combine_ln_kernel(gate_ref, x_ref, y0_ref, y1_ref, y2_ref, y3_ref, g_ref, b_ref, o_ref):
    gate = gate_ref[...]
    f = jnp.zeros(x_ref.shape, F32)
    for j, y_ref in enumerate((y0_ref, y1_ref, y2_ref, y3_ref)):
        f = f + y_ref[...] * gate[:, j:j + 1]
    o_ref[...] = _layer_norm(DN_ALPHA * x_ref[...] + f, g_ref[...], b_ref[...])


def _combine_ln(xt, y4, gates, g, b):
    N, D = xt.shape
    T = MOE_TOK_TILE
    nt = N // T
    yspec = lambda j: pl.BlockSpec((T, D), lambda i: (j * nt + i, 0))
    return pl.pallas_call(
        _combine_ln_kernel,
        grid=(nt,),
        in_specs=[
            pl.BlockSpec((T, TOP_K), lambda i: (i, 0)),
            pl.BlockSpec((T, D), lambda i: (i, 0)),
            yspec(0), yspec(1), yspec(2), yspec(3),
            pl.BlockSpec((1, D), lambda i: (0, 0)),
            pl.BlockSpec((1, D), lambda i: (0, 0)),
        ],
        out_specs=pl.BlockSpec((T, D), lambda i: (i, 0)),
        out_shape=jax.ShapeDtypeStruct((N, D), F32),
        compiler_params=_cparams(("parallel",)),
        name="moe_combine_ln",
    )(gates, xt, y4, y4, y4, y4, g.reshape(1, D), b.reshape(1, D))


def _moe_ln(xt, w_router, b_router, w1, b1, w2, b2, g, b, layer):
    N, D = xt.shape
    idx, gates, rank, counts = _router(xt, w_router, b_router)
    counts = counts.reshape(N_EXPERTS).astype(jnp.int32)
    padded = ((counts + MOE_BLOCK - 1) // MOE_BLOCK) * MOE_BLOCK
    pend = jnp.cumsum(padded)
    pstart = pend - padded
    n_blocks = -(-(N * TOP_K) // MOE_BLOCK) + N_EXPERTS
    hot = idx[:, :, None] == jnp.arange(N_EXPERTS, dtype=jnp.int32)
    dest = jnp.sum(jnp.where(hot, pstart, 0), axis=-1) + rank
    blk_start = jnp.arange(n_blocks, dtype=jnp.int32) * MOE_BLOCK
    blk_e = jnp.minimum(jnp.sum((pend[None, :] <= blk_start[:, None]).astype(jnp.int32), axis=1),
                        N_EXPERTS - 1)
    n_used = (pend[-1:] // MOE_BLOCK).astype(jnp.int32)
    row_map = _rowmap(dest.T.reshape(N * TOP_K), N, n_blocks)
    y4 = _experts(xt, row_map, blk_e, n_used, w1, b1, w2, b2, layer)
    return _combine_ln(xt, y4, gates.T, g, b)


def _dsa_layer(x, w_in, kv_norm, w_uk, w_uv, w_out, g, b):
    Bsz, L, D = x.shape
    qlat, c, ct, qi, ki, wi = _dsa_proj(x, w_in, kv_norm, w_uk)
    o = _dsa_attention(qi, wi, ki, qlat, ct, c, w_uv)
    return _outproj_ln(o.reshape(Bsz * L, -1), w_out, x.reshape(Bsz * L, D), g, b)


def _gla_layer(x, w_in, w_g2, g_bias, norm_g, w_out, g, b):
    Bsz, L, D = x.shape
    xt = x.reshape(Bsz * L, D)
    q, k, v, la, r = _gla_proj(xt, w_in, w_g2, g_bias)
    sh = lambda t: t.reshape(Bsz, L, t.shape[-1])
    o = _gla_core(sh(q), sh(k), sh(v), sh(la), sh(r), norm_g)
    return _outproj_ln(o.reshape(Bsz * L, -1), w_out, xt, g, b)


def kernel(x, a_w_in, a_kv_norm, a_w_uk, a_w_uv, a_w_out, b_w_in, b_w_g2, b_g_bias, b_norm, b_w_out, m_w_router, m_b_router, m_w1, m_b1, m_w2, m_b2, ln1_g, ln1_b, ln2_g, ln2_b):
    Bsz, L, D = x.shape
    for i in range(DEPTH):
        j = i // 2
        if i % 2 == 0:
            xt = _dsa_layer(x, a_w_in[j], a_kv_norm[j], a_w_uk[j], a_w_uv[j], a_w_out[j],
                            ln1_g[i], ln1_b[i])
        else:
            xt = _gla_layer(x, b_w_in[j], b_w_g2[j], b_g_bias[j], b_norm[j], b_w_out[j],
                            ln1_g[i], ln1_b[i])
        xt = _moe_ln(xt, m_w_router[i], m_b_router[i], m_w1, m_b1, m_w2, m_b2,
                     ln2_g[i], ln2_b[i], i)
        x = xt.reshape(Bsz, L, D)
    return x
```

```python
import functools

import jax
import jax.numpy as jnp
from jax import lax
from jax.experimental import pallas as pl
from jax.experimental.pallas import tpu as pltpu

F32 = jnp.float32
BF16 = jnp.bfloat16

DEPTH = 2
DN_ALPHA = (2.0 * DEPTH) ** 0.25
LN_EPS = 1e-5
RMS_EPS = 1e-6

A_HEADS = 16
A_QK_DIM = 64
A_V_DIM = 64
A_LATENT = 256
IDX_HEADS = 8
IDX_DIM = 64
IDX_TOPK_MAX = 256
A_QBLK = 256
A_KBLK = 512
A_ONES = 16
CNT_ROWS = 32
BIS_STEPS = 4
BIS_MAX_STEPS = 36
BIS_FIRST_PIVOT = 0.0625
LOG2E = 1.4426950408889634

B_HEADS = 4
B_GATE_RANK = 16
B_GATE_TAU = 16.0
B_CHUNK = 64
B_SUB = 16
B_TBLK = 512

N_EXPERTS = 32
TOP_K = 4
SWIGLU_LIMIT = 7.0
SWIGLU_ALPHA = 1.702
MOE_BLOCK = 256
MOE_CHUNK = 256
ROW_TILE = 512
MOE_TOK_TILE = 256

VMEM_LIMIT = 56 * 1024 * 1024
NEG_BIG = -1e30


def _cparams(sem):
    return pltpu.CompilerParams(dimension_semantics=sem, vmem_limit_bytes=VMEM_LIMIT)


def _dot(a, b):
    return jnp.dot(a, b, preferred_element_type=F32)


def _dot_nt(a, b):
    return lax.dot_general(a, b, (((1,), (1,)), ((), ())), preferred_element_type=F32)


def _dot_tn(a, b):
    return lax.dot_general(a, b, (((0,), (0,)), ((), ())), preferred_element_type=F32)


def _layer_norm(z, g, b):
    mu = jnp.mean(z, axis=-1, keepdims=True)
    zc = z - mu
    var = jnp.mean(zc * zc, axis=-1, keepdims=True)
    return zc * lax.rsqrt(var + LN_EPS) * g + b


def _dsa_proj_kernel(x_ref, wqt_ref, wc_ref, wqit_ref, wki_ref, wwit_ref, kvn_ref, wukt_ref,
                     qlat_ref, c_ref, ct_ref, qi_ref, ki_ref, wi_ref):
    T = x_ref.shape[0]
    QB = A_QBLK
    xb = x_ref[...].astype(BF16)
    qt = _dot_nt(wqt_ref[...], xb)
    for h in range(A_HEADS):
        qh = qt[h * A_QK_DIM:(h + 1) * A_QK_DIM, :].astype(BF16)
        ql = (_dot(wukt_ref[h], qh) * (A_QK_DIM ** -0.5 * LOG2E)).astype(BF16)
        for j in range(T // QB):
            qlat_ref[j, :, h * QB:(h + 1) * QB] = ql[:, j * QB:(j + 1) * QB]
    c = _dot(xb, wc_ref[...])
    c = c * lax.rsqrt(jnp.mean(c * c, axis=-1, keepdims=True) + RMS_EPS) * kvn_ref[...]
    c_ref[...] = c.astype(BF16)
    ct_ref[:A_LATENT, :] = c.T.astype(BF16)
    ct_ref[A_LATENT:, :] = jnp.ones((A_ONES, T), BF16)
    qit = _dot_nt(wqit_ref[...], xb) * (IDX_DIM ** -0.5)
    for h in range(IDX_HEADS):
        qh = qit[h * IDX_DIM:(h + 1) * IDX_DIM, :].astype(BF16)
        for j in range(T // QB):
            qi_ref[j, :, h * QB:(h + 1) * QB] = qh[:, j * QB:(j + 1) * QB]
    ki_ref[...] = _dot(xb, wki_ref[...]).astype(BF16)
    wi_ref[...] = _dot_nt(wwit_ref[...], xb) * (IDX_HEADS ** -0.5)


def _dsa_proj(x, w_in, kv_norm, w_uk):
    Bsz, L, D = x.shape
    T, QB = A_KBLK, A_QBLK
    nkb, nqb, qpt = L // T, L // QB, T // QB
    o1 = A_HEADS * A_QK_DIM
    o2 = o1 + A_LATENT
    o3 = o2 + IDX_HEADS * IDX_DIM
    o4 = o3 + IDX_DIM
    wb = w_in.astype(BF16)
    wqt, wc, wqit, wki, wwit = wb[:, :o1].T, wb[:, o1:o2], wb[:, o2:o3].T, wb[:, o3:o4], wb[:, o4:].T
    wukt = jnp.swapaxes(w_uk, 1, 2).astype(BF16)
    full = lambda shape: pl.BlockSpec(shape, lambda b, t: (0,) * len(shape))
    return pl.pallas_call(
        _dsa_proj_kernel,
        grid=(Bsz, nkb),
        in_specs=[
            pl.BlockSpec((None, T, D), lambda b, t: (b, t, 0)),
            full(wqt.shape), full(wc.shape), full(wqit.shape), full(wki.shape), full(wwit.shape),
            full((1, A_LATENT)), full(wukt.shape),
        ],
        out_specs=[
            pl.BlockSpec((None, qpt, A_LATENT, A_HEADS * QB), lambda b, t: (b, t, 0, 0)),
            pl.BlockSpec((None, None, T, A_LATENT), lambda b, t: (b, t, 0, 0)),
            pl.BlockSpec((None, None, A_LATENT + A_ONES, T), lambda b, t: (b, t, 0, 0)),
            pl.BlockSpec((None, qpt, IDX_DIM, IDX_HEADS * QB), lambda b, t: (b, t, 0, 0)),
            pl.BlockSpec((None, None, T, IDX_DIM), lambda b, t: (b, t, 0, 0)),
            pl.BlockSpec((None, IDX_HEADS, T), lambda b, t: (b, 0, t)),
        ],
        out_shape=[
            jax.ShapeDtypeStruct((Bsz, nqb, A_LATENT, A_HEADS * QB), BF16),
            jax.ShapeDtypeStruct((Bsz, nkb, T, A_LATENT), BF16),
            jax.ShapeDtypeStruct((Bsz, nkb, A_LATENT + A_ONES, T), BF16),
            jax.ShapeDtypeStruct((Bsz, nqb, IDX_DIM, IDX_HEADS * QB), BF16),
            jax.ShapeDtypeStruct((Bsz, nkb, T, IDX_DIM), BF16),
            jax.ShapeDtypeStruct((Bsz, IDX_HEADS, L), F32),
        ],
        compiler_params=_cparams(("parallel", "parallel")),
        name="dsa_proj",
    )(x, wqt, wc, wqit, wki, wwit, kv_norm.reshape(1, A_LATENT), wukt)


def _ordered_int_to_float(o):
    bits = jnp.where(o >= 0, o, o ^ jnp.int32(2 ** 31 - 1))
    return lax.bitcast_convert_type(bits, F32)


def _float_to_ordered_int(x):
    bits = lax.bitcast_convert_type(jnp.asarray(x, F32), jnp.int32)
    return jnp.where(bits >= 0, bits, bits ^ jnp.int32(2 ** 31 - 1))


def _dsa_attn_kernel(qi_ref, wi_ref, ki_ref, qlat_ref, ct_ref, c_ref, wuvt_ref, o_ref,
                     sc_ref, s_ref, m_ref, acc_ref, ot_ref, jcut_ref, *, topk, seq_len):
    QB, KB = A_QBLK, A_KBLK
    qb = pl.program_id(1)
    q0 = qb * QB
    nvb = (q0 + QB + KB - 1) // KB
    qpos = q0 + lax.broadcasted_iota(jnp.int32, (1, QB), 1)
    krow = lax.broadcasted_iota(jnp.int32, (KB, QB), 0)
    kf = float(topk)

    wi = wi_ref[...]
    qi = qi_ref[...]

    def score_body(kb, carry):
        s = _dot(ki_ref[kb], qi)
        sc = jnp.zeros((KB, QB), F32)
        for h in range(IDX_HEADS):
            sc = sc + jnp.maximum(s[:, h * QB:(h + 1) * QB], 0.0) * wi[h:h + 1, :]
        sc_ref[kb] = jnp.where((kb * KB + krow) <= qpos, sc, -jnp.inf)
        return carry

    lax.fori_loop(0, nvb, score_body, 0)

    def count(pred):
        def body(kb, acc):
            hit = jnp.where(pred(sc_ref[kb], kb), 1.0, 0.0)
            return acc + jnp.sum(hit.reshape(KB // CNT_ROWS, CNT_ROWS, QB), axis=0)
        acc = lax.fori_loop(0, nvb, body, jnp.zeros((CNT_ROWS, QB), F32))
        return jnp.sum(acc, axis=0, keepdims=True)

    def row_max():
        def body(kb, acc):
            return jnp.maximum(acc, jnp.max(sc_ref[kb].reshape(KB // CNT_ROWS, CNT_ROWS, QB), axis=0))
        acc = lax.fori_loop(0, nvb, body, jnp.full((CNT_ROWS, QB), -jnp.inf, F32))
        return jnp.max(acc, axis=0, keepdims=True)

    def half_gap(lo, hi):
        return lax.shift_right_logical(hi - lo, jnp.int32(1))

    def bis_step(st, mid):
        lo, hi, thr, done = st
        cand = _ordered_int_to_float(mid)
        cnt = count(lambda blk, kb: blk >= cand)
        ge = cnt >= kf
        hit = (cnt == kf) & (done < 0.5)
        return (jnp.where(ge, mid, lo), jnp.where(ge, hi, mid),
                jnp.where(hit, cand, thr), jnp.where(hit, 1.0, done))

    def bis_cond(st):
        i, lo, hi, _, done = st
        open_ = jnp.where((done < 0.5) & (half_gap(lo, hi) > 0), 1.0, 0.0)
        return (i < BIS_MAX_STEPS) & (jnp.max(open_) > 0.5)

    def bis_body(st):
        i, st = st[0], st[1:]
        for _ in range(BIS_STEPS):
            st = bis_step(st, st[0] + half_gap(st[0], st[1]))
        return (i + BIS_STEPS,) + st

    smax = row_max()
    lo0 = jnp.full((1, QB), _float_to_ordered_int(-jnp.inf), jnp.int32)
    hi0 = _float_to_ordered_int(smax) + 1
    done0 = jnp.where(qpos < topk, 1.0, 0.0)
    st = (lo0, hi0, jnp.full((1, QB), -jnp.inf, F32), done0)
    first = jnp.where(smax > 0.0, _float_to_ordered_int(smax * BIS_FIRST_PIVOT),
                      lo0 + half_gap(lo0, hi0))
    st = bis_step(st, first)
    _, u, _, thr, done = lax.while_loop(bis_cond, bis_body, (jnp.int32(0),) + st)
    thr = jnp.where(done > 0.5, thr, _ordered_int_to_float(u))
    jcut_ref[...] = jnp.full((1, QB), seq_len, jnp.int32)

    @pl.when(jnp.min(done) < 0.5)
    def _():
        need = kf - count(lambda blk, kb: blk > thr)
        nbits = max(1, (seq_len - 1).bit_length())

        def jbody(i, x):
            trial = x | jnp.left_shift(jnp.int32(1), nbits - 1 - i)
            g = count(lambda blk, kb: (blk == thr) & ((kb * KB + krow) < trial))
            return jnp.where(g < need, trial, x)

        x = lax.fori_loop(0, nbits, jbody, jnp.zeros((1, QB), jnp.int32))
        jcut_ref[...] = jnp.where(done > 0.5, seq_len, x)

    jcut = jcut_ref[...]

    m_ref[...] = jnp.full(m_ref.shape, NEG_BIG, F32)
    acc_ref[...] = jnp.zeros(acc_ref.shape, F32)

    pair_cols = [slice(hp * 2 * QB, (hp + 1) * 2 * QB) for hp in range(A_HEADS // 2)]

    def logits(kb, cols):
        return _dot(c_ref[kb], qlat_ref[:, cols])

    def mask_bias(kb):
        blk = sc_ref[kb]
        kpos = kb * KB + krow
        sel = ((blk > thr) | ((blk == thr) & (kpos <= jcut))) & (kpos <= qpos)
        bias = jnp.where(sel, 0.0, NEG_BIG)
        return jnp.concatenate([bias, bias], axis=1)

    def consume(kb, s, cols, bias2):
        s = s + bias2
        m_prev = m_ref[:, cols]
        m_new = jnp.maximum(m_prev, jnp.max(s, axis=0, keepdims=True))
        alpha = jnp.exp2(m_prev - m_new)
        p = jnp.exp2(s - m_new).astype(BF16)
        acc_ref[:, cols] = alpha * acc_ref[:, cols] + _dot(ct_ref[kb], p)
        m_ref[:, cols] = m_new

    for cols in pair_cols:
        s_ref[:, cols] = logits(0, cols)

    def attn_body(kb, carry):
        bias2 = mask_bias(kb)
        for cols in pair_cols:
            s = s_ref[:, cols]
            s_ref[:, cols] = logits(kb + 1, cols)
            consume(kb, s, cols, bias2)
        return carry

    lax.fori_loop(0, nvb - 1, attn_body, 0)
    bias2 = mask_bias(nvb - 1)
    for cols in pair_cols:
        consume(nvb - 1, s_ref[:, cols], cols, bias2)

    for h in range(A_HEADS):
        cols = slice(h * QB, (h + 1) * QB)
        oh = acc_ref[:A_LATENT, cols] / acc_ref[A_LATENT:A_LATENT + 1, cols]
        ot_ref[h * A_V_DIM:(h + 1) * A_V_DIM, :] = _dot(wuvt_ref[h], oh.astype(BF16))
    o_ref[...] = ot_ref[...].T.astype(BF16)


def _dsa_attention(qi, wi, ki, qlat, ct, c, w_uv):
    Bsz, nqb, _, _ = qlat.shape
    QB, KB = A_QBLK, A_KBLK
    L = nqb * QB
    nkb = L // KB
    topk = min(IDX_TOPK_MAX, L // 4)
    wuvt = jnp.swapaxes(w_uv, 1, 2).astype(BF16)
    kern = functools.partial(_dsa_attn_kernel, topk=topk, seq_len=L)
    return pl.pallas_call(
        kern,
        grid=(Bsz, nqb),
        in_specs=[
            pl.BlockSpec((None, None, IDX_DIM, IDX_HEADS * QB), lambda b, q: (b, q, 0, 0)),
            pl.BlockSpec((None, IDX_HEADS, QB), lambda b, q: (b, 0, q)),
            pl.BlockSpec((None, nkb, KB, IDX_DIM), lambda b, q: (b, 0, 0, 0)),
            pl.BlockSpec((None, None, A_LATENT, A_HEADS * QB), lambda b, q: (b, q, 0, 0)),
            pl.BlockSpec((None, nkb, A_LATENT + A_ONES, KB), lambda b, q: (b, 0, 0, 0)),
            pl.BlockSpec((None, nkb, KB, A_LATENT), lambda b, q: (b, 0, 0, 0)),
            pl.BlockSpec((A_HEADS, A_V_DIM, A_LATENT), lambda b, q: (0, 0, 0)),
        ],
        out_specs=pl.BlockSpec((None, QB, A_HEADS * A_V_DIM), lambda b, q: (b, q, 0)),
        out_shape=jax.ShapeDtypeStruct((Bsz, L, A_HEADS * A_V_DIM), BF16),
        scratch_shapes=[
            pltpu.VMEM((nkb, KB, QB), F32),
            pltpu.VMEM((KB, A_HEADS * QB), F32),
            pltpu.VMEM((1, A_HEADS * QB), F32),
            pltpu.VMEM((A_LATENT + A_ONES, A_HEADS * QB), F32),
            pltpu.VMEM((A_HEADS * A_V_DIM, QB), F32),
            pltpu.VMEM((1, QB), jnp.int32),
        ],
        compiler_params=_cparams(("parallel", "parallel")),
        name="dsa_attn",
    )(qi, wi, ki, qlat, ct, c, wuvt)


def _outproj_ln_kernel(o_ref, w_ref, x_ref, g_ref, b_ref, y_ref):
    h = _dot(o_ref[...], w_ref[...])
    y_ref[...] = _layer_norm(DN_ALPHA * x_ref[...] + h, g_ref[...], b_ref[...])


def _outproj_ln(o, w_out, x, g, b):
    N, K = o.shape
    D = x.shape[1]
    T = ROW_TILE
    return pl.pallas_call(
        _outproj_ln_kernel,
        grid=(N // T,),
        in_specs=[
            pl.BlockSpec((T, K), lambda i: (i, 0)),
            pl.BlockSpec((K, D), lambda i: (0, 0)),
            pl.BlockSpec((T, D), lambda i: (i, 0)),
            pl.BlockSpec((1, D), lambda i: (0, 0)),
            pl.BlockSpec((1, D), lambda i: (0, 0)),
        ],
        out_specs=pl.BlockSpec((T, D), lambda i: (i, 0)),
        out_shape=jax.ShapeDtypeStruct((N, D), F32),
        compiler_params=_cparams(("parallel",)),
        name="outproj_ln",
    )(o, w_out.astype(BF16), x, g.reshape(1, D), b.reshape(1, D))


def _gla_proj_kernel(x_ref, wq_ref, wk_ref, wv_ref, wg_ref, wr_ref, wg2_ref, gb_ref,
                     q_ref, k_ref, v_ref, la_ref, r_ref, *, kh):
    xb = x_ref[...].astype(BF16)
    q_ref[...] = _dot(xb, wq_ref[...]) * (kh ** -0.5)
    k_ref[...] = _dot(xb, wk_ref[...])
    v_ref[...] = _dot(xb, wv_ref[...]).astype(BF16)
    r_ref[...] = _dot(xb, wr_ref[...])
    glr = _dot(xb, wg_ref[...])
    z = _dot(glr.astype(BF16), wg2_ref[...]) + gb_ref[...]
    log_sig = jnp.minimum(z, 0.0) - jnp.log(1.0 + jnp.exp(-jnp.abs(z)))
    la_ref[...] = log_sig / B_GATE_TAU


def _gla_proj(x, w_in, w_g2, g_bias):
    N, D = x.shape
    DK = w_g2.shape[1]
    DV = D
    T = ROW_TILE
    wb = w_in.astype(BF16)
    o1, o2, o3, o4 = DK, 2 * DK, 2 * DK + DV, 2 * DK + DV + B_GATE_RANK
    wq, wk, wv, wg, wr = wb[:, :o1], wb[:, o1:o2], wb[:, o2:o3], wb[:, o3:o4], wb[:, o4:]
    full = lambda shape: pl.BlockSpec(shape, lambda i: (0,) * len(shape))
    row = lambda w: pl.BlockSpec((T, w), lambda i: (i, 0))
    kern = functools.partial(_gla_proj_kernel, kh=DK // B_HEADS)
    return pl.pallas_call(
        kern,
        grid=(N // T,),
        in_specs=[row(D), full(wq.shape), full(wk.shape), full(wv.shape), full(wg.shape),
                  full(wr.shape), full(w_g2.shape), full((1, DK))],
        out_specs=[row(DK), row(DK), row(DV), row(DK), row(DV)],
        out_shape=[
            jax.ShapeDtypeStruct((N, DK), F32),
            jax.ShapeDtypeStruct((N, DK), F32),
            jax.ShapeDtypeStruct((N, DV), BF16),
            jax.ShapeDtypeStruct((N, DK), F32),
            jax.ShapeDtypeStruct((N, DV), F32),
        ],
        compiler_params=_cparams(("parallel",)),
        name="gla_proj",
    )(x, wq, wk, wv, wg, wr, w_g2.astype(BF16), g_bias.reshape(1, DK))


def _split3(a):
    hi = a.astype(BF16)
    r1 = a - hi.astype(F32)
    mid = r1.astype(BF16)
    lo = (r1 - mid.astype(F32)).astype(BF16)
    return hi, mid, lo


def _gla_kernel(q_ref, k_ref, v_ref, la_ref, r_ref, ng_ref, o_ref, st_ref, a_ref):
    C, SB = B_CHUNK, B_SUB
    nsub = C // SB
    kh = q_ref.shape[1] // B_HEADS
    vh = v_ref.shape[1] // B_HEADS

    @pl.when(pl.program_id(1) == 0)
    def _():
        st_ref[...] = jnp.zeros(st_ref.shape, F32)

    ri = lax.broadcasted_iota(jnp.int32, (C, C), 0)
    ci = lax.broadcasted_iota(jnp.int32, (C, C), 1)
    tril = jnp.where(ci <= ri, 1.0, 0.0).astype(BF16)
    sub_r = lax.broadcasted_iota(jnp.int32, (SB, 1), 0)
    sub_c = lax.broadcasted_iota(jnp.int32, (1, SB), 1)

    def head_chunk(rows, h):
        kc = slice(h * kh, (h + 1) * kh)
        vc = slice(h * vh, (h + 1) * vh)
        q = q_ref[rows, kc]
        k = k_ref[rows, kc]
        v = v_ref[rows, vc]
        la = la_ref[rows, kc]
        hi, mid, lo = _split3(la)
        b = _dot(tril, hi) + _dot(tril, mid) + _dot(tril, lo)
        st = st_ref[h]
        o = _dot_nt((q * jnp.exp(b)).astype(BF16), st.astype(BF16))

        a_ref[h] = jnp.zeros((C, C), F32)
        for I in range(nsub):
            r0 = I * SB
            bI = b[r0:r0 + SB, :]
            qI = q[r0:r0 + SB, :]
            kI = k[r0:r0 + SB, :]
            if I > 0:
                ref_lvl = b[r0 - 1:r0, :]
                qs = (qI * jnp.exp(bI - ref_lvl)).astype(BF16)
                ks = (k[:r0, :] * jnp.exp(ref_lvl - b[:r0, :])).astype(BF16)
                a_ref[h, r0:r0 + SB, :r0] = _dot_nt(qs, ks)
            diag = jnp.zeros((SB, SB), F32)
            for j in range(SB):
                dlt = jnp.where(sub_r >= j, bI - bI[j:j + 1, :], -jnp.inf)
                col = jnp.sum(qI * kI[j:j + 1, :] * jnp.exp(dlt), axis=1, keepdims=True)
                diag = diag + col * jnp.where(sub_c == j, 1.0, 0.0)
            a_ref[h, r0:r0 + SB, r0:r0 + SB] = diag
        o = o + _dot(a_ref[h].astype(BF16), v)

        b_last = b[C - 1:C, :]
        kd = (k * jnp.exp(b_last - b)).astype(BF16)
        st_ref[h] = st * jnp.exp(b_last) + _dot_tn(v, kd)

        o = o * lax.rsqrt(jnp.mean(o * o, axis=-1, keepdims=True) + RMS_EPS) * ng_ref[:, vc]
        r = r_ref[rows, vc]
        o_ref[rows, vc] = (o * (r / (1.0 + jnp.exp(-r)))).astype(BF16)

    def chunk_body(ch, carry):
        rows = pl.ds(pl.multiple_of(ch * C, C), C)
        for h in range(B_HEADS):
            head_chunk(rows, h)
        return carry

    lax.fori_loop(0, q_ref.shape[0] // C, chunk_body, 0)


def _gla_core(q, k, v, la, r, norm_g):
    Bsz, L, DK = q.shape
    DV = v.shape[2]
    kh, vh = DK // B_HEADS, DV // B_HEADS
    T = B_TBLK
    kspec = pl.BlockSpec((None, T, DK), lambda b, t: (b, t, 0))
    vspec = pl.BlockSpec((None, T, DV), lambda b, t: (b, t, 0))
    return pl.pallas_call(
        _gla_kernel,
        grid=(Bsz, L // T),
        in_specs=[kspec, kspec, vspec, kspec, vspec, pl.BlockSpec((1, DV), lambda b, t: (0, 0))],
        out_specs=vspec,
        out_shape=jax.ShapeDtypeStruct((Bsz, L, DV), BF16),
        scratch_shapes=[pltpu.VMEM((B_HEADS, vh, kh), F32),
                        pltpu.VMEM((B_HEADS, B_CHUNK, B_CHUNK), F32)],
        compiler_params=_cparams(("parallel", "arbitrary")),
        name="gla_core",
    )(q, k, v, la, r, norm_g.reshape(1, DV))


def _router_kernel(x_ref, wt_ref, b_ref, idx_ref, gate_ref, rank_ref, cnt_ref, carry_ref):
    T = x_ref.shape[0]
    E = N_EXPERTS

    @pl.when(pl.program_id(0) == 0)
    def _():
        carry_ref[...] = jnp.zeros(carry_ref.shape, F32)

    x = x_ref[...]
    wt = wt_ref[...]
    xh = x.astype(BF16)
    xl = (x - xh.astype(F32)).astype(BF16)
    wh = wt.astype(BF16)
    wl = (wt - wh.astype(F32)).astype(BF16)
    logits = _dot_nt(wh, xh) + (_dot_nt(wl, xh) + _dot_nt(wh, xl)) + b_ref[...]

    eidx = lax.broadcasted_iota(jnp.int32, (E, T), 0)
    work = logits
    vals, idxs = [], []
    for _ in range(TOP_K):
        mx = jnp.max(work, axis=0, keepdims=True)
        ix = jnp.min(jnp.where(work == mx, eidx, E), axis=0, keepdims=True)
        vals.append(mx)
        idxs.append(ix)
        work = jnp.where(eidx == ix, -jnp.inf, work)
    exps = [jnp.exp(vv - vals[0]) for vv in vals]
    den = exps[0] + exps[1] + exps[2] + exps[3]

    onehot = jnp.zeros((E, T), F32)
    for ix in idxs:
        onehot = onehot + jnp.where(eidx == ix, 1.0, 0.0)
    ri = lax.broadcasted_iota(jnp.int32, (T, T), 0)
    ci = lax.broadcasted_iota(jnp.int32, (T, T), 1)
    before = jnp.where(ri < ci, 1.0, 0.0).astype(BF16)
    prefix = _dot(onehot.astype(BF16), before) + carry_ref[...]
    carry_ref[...] = carry_ref[...] + jnp.sum(onehot, axis=1, keepdims=True)
    cnt_ref[...] = carry_ref[...]

    for j in range(TOP_K):
        idx_ref[j:j + 1, :] = idxs[j]
        gate_ref[j:j + 1, :] = exps[j] / den
        rk = jnp.sum(jnp.where(eidx == idxs[j], prefix, 0.0), axis=0, keepdims=True)
        rank_ref[j:j + 1, :] = rk.astype(jnp.int32)


def _router(xt, w_router, b_router):
    N, D = xt.shape
    T = MOE_TOK_TILE
    tok = pl.BlockSpec((TOP_K, T), lambda i: (0, i))
    return pl.pallas_call(
        _router_kernel,
        grid=(N // T,),
        in_specs=[
            pl.BlockSpec((T, D), lambda i: (i, 0)),
            pl.BlockSpec((N_EXPERTS, D), lambda i: (0, 0)),
            pl.BlockSpec((N_EXPERTS, 1), lambda i: (0, 0)),
        ],
        out_specs=[tok, tok, tok, pl.BlockSpec((N_EXPERTS, 1), lambda i: (0, 0))],
        out_shape=[
            jax.ShapeDtypeStruct((TOP_K, N), jnp.int32),
            jax.ShapeDtypeStruct((TOP_K, N), F32),
            jax.ShapeDtypeStruct((TOP_K, N), jnp.int32),
            jax.ShapeDtypeStruct((N_EXPERTS, 1), F32),
        ],
        scratch_shapes=[pltpu.VMEM((N_EXPERTS, 1), F32)],
        compiler_params=_cparams(("arbitrary",)),
        name="moe_router",
    )(xt, w_router.T, b_router.reshape(N_EXPERTS, 1))


def _rowmap_kernel(dest_ref, init_ref, map_ref, sem, *, n_tok):
    i = pl.program_id(0)
    toks = dest_ref.shape[0] // TOP_K

    @pl.when(i == 0)
    def _():
        cp = pltpu.make_async_copy(init_ref, map_ref, sem)
        cp.start()
        cp.wait()

    t0 = i * toks

    def body(a, c):
        for j in range(TOP_K):
            map_ref[dest_ref[a * TOP_K + j]] = j * n_tok + t0 + a
        return c

    lax.fori_loop(0, toks, body, 0, unroll=8)


def _rowmap(dest_flat, n_tok, n_blocks):
    NK = dest_flat.shape[0]
    tile = min(NK, 8192)
    P = n_blocks * MOE_BLOCK
    init = n_tok * TOP_K + (jnp.arange(P, dtype=jnp.int32) & (MOE_BLOCK - 1))
    return pl.pallas_call(
        functools.partial(_rowmap_kernel, n_tok=n_tok),
        grid=(NK // tile,),
        in_specs=[pl.BlockSpec((tile,), lambda i: (i,), memory_space=pltpu.SMEM),
                  pl.BlockSpec(memory_space=pl.ANY)],
        out_specs=pl.BlockSpec((P,), lambda i: (0,), memory_space=pltpu.SMEM),
        out_shape=jax.ShapeDtypeStruct((P,), jnp.int32),
        scratch_shapes=[pltpu.SemaphoreType.DMA],
        compiler_params=_cparams(("arbitrary",)),
        name="moe_rowmap",
    )(dest_flat, init)


def _expert_kernel(blk_e_ref, nused_ref, map_prev_ref, map_cur_ref, map_next_ref,
                   x_ref, w1_ref, b1_ref, w2_ref, b2_ref, y4_ref,
                   xbuf0, xbuf1, ybuf0, ybuf1, act_ref, w1b_ref, w2b_ref, gsem, ssem, *, n_tok):
    i = pl.program_id(0)
    n_used = nused_ref[0]
    BLK = MOE_BLOCK
    spare_row0 = n_tok * TOP_K

    def gather_row(map_ref, r, xdst):
        tok = map_ref[0, r] & (n_tok - 1)
        return pltpu.make_async_copy(x_ref.at[pl.ds(tok, 1)], xdst.at[pl.ds(r, 1)], gsem)

    def scatter_row(dst, r, ysrc):
        return pltpu.make_async_copy(ysrc.at[pl.ds(r, 1)], y4_ref.at[pl.ds(dst, 1)], ssem)

    def wait_rows(sem):
        pltpu.make_async_copy(xbuf0, xbuf1, sem).wait()

    @pl.when(i == 0)
    def _():
        ybuf1[...] = jnp.zeros(ybuf1.shape, F32)

        def first(r, c):
            gather_row(map_cur_ref, r, xbuf0).start()
            return c
        lax.fori_loop(0, BLK, first, 0, unroll=8)

    prev = blk_e_ref[jnp.maximum(i - 1, 0)]
    fresh = (i == 0) | (blk_e_ref[i] != prev)

    @pl.when(fresh & (i < n_used))
    def _():
        w1b_ref[...] = w1_ref[...].astype(BF16)
        w2b_ref[...] = w2_ref[...].astype(BF16)

    def step(xcur, xnext, ycur, yprev):
        wait_rows(gsem)

        @pl.when(i >= 1)
        def _():
            wait_rows(ssem)

        F = w2_ref.shape[0]
        D = w2_ref.shape[1]
        n1, n2 = F // MOE_CHUNK, D // MOE_CHUNK
        rows_per_group = BLK // (n1 + n2)

        def neighbour_dmas(group):
            for r in range(group * rows_per_group, (group + 1) * rows_per_group):
                gather_row(map_next_ref, r, xnext).start()
                dst = jnp.where(i == 0, spare_row0 + r, map_prev_ref[0, r])
                scatter_row(dst, r, yprev).start()

        xb = xcur[...].astype(BF16)
        for c in range(n1):
            gc = slice(c * MOE_CHUNK, (c + 1) * MOE_CHUNK)
            uc = slice(F + c * MOE_CHUNK, F + (c + 1) * MOE_CHUNK)
            g = jnp.minimum(_dot(xb, w1b_ref[:, gc]) + b1_ref[:, gc], SWIGLU_LIMIT)
            u = jnp.clip(_dot(xb, w1b_ref[:, uc]) + b1_ref[:, uc], -SWIGLU_LIMIT, SWIGLU_LIMIT)
            glu = g / (1.0 + jnp.exp(-SWIGLU_ALPHA * g))
            act_ref[:, gc] = ((u + 1.0) * glu).astype(BF16)
            neighbour_dmas(c)
        act = act_ref[...]
        for c in range(n2):
            oc = slice(c * MOE_CHUNK, (c + 1) * MOE_CHUNK)
            ycur[:, oc] = _dot(act, w2b_ref[:, oc]) + b2_ref[:, oc]
            neighbour_dmas(n1 + c)

        @pl.when(i == n_used - 1)
        def _():
            wait_rows(gsem)
            wait_rows(ssem)

            def last(r, c):
                scatter_row(map_cur_ref[0, r], r, ycur).start()
                return c
            lax.fori_loop(0, BLK, last, 0, unroll=8)
            wait_rows(ssem)

    @pl.when((i < n_used) & (i % 2 == 0))
    def _():
        step(xbuf0, xbuf1, ybuf0, ybuf1)

    @pl.when((i < n_used) & (i % 2 == 1))
    def _():
        step(xbuf1, xbuf0, ybuf1, ybuf0)


def _experts(x, row_map, blk_e, n_used, w1, b1, w2, b2, layer):
    N, D = x.shape
    assert N & (N - 1) == 0, "token count must be a power of two (row map decode)"
    nb = row_map.shape[0] // MOE_BLOCK
    _, E, _, F2 = w1.shape
    F = w2.shape[2]
    map3 = row_map.reshape(nb, 1, MOE_BLOCK)
    wspec = lambda r, c: pl.BlockSpec((None, None, r, c), lambda i, be, nu: (layer, be[i], 0, 0))
    mspec = lambda off: pl.BlockSpec(
        (None, 1, MOE_BLOCK), lambda i, be, nu: (jnp.clip(i + off, 0, nb - 1), 0, 0),
        memory_space=pltpu.SMEM)
    grid_spec = pltpu.PrefetchScalarGridSpec(
        num_scalar_prefetch=2,
        grid=(nb,),
        in_specs=[
            mspec(-1), mspec(0), mspec(1),
            pl.BlockSpec(memory_space=pl.ANY),
            wspec(D, F2), wspec(1, F2), wspec(F, D), wspec(1, D),
        ],
        out_specs=pl.BlockSpec(memory_space=pl.ANY),
        scratch_shapes=[pltpu.VMEM((MOE_BLOCK, D), F32), pltpu.VMEM((MOE_BLOCK, D), F32),
                        pltpu.VMEM((MOE_BLOCK, D), F32), pltpu.VMEM((MOE_BLOCK, D), F32),
                        pltpu.VMEM((MOE_BLOCK, F), BF16),
                        pltpu.VMEM((D, F2), BF16), pltpu.VMEM((F, D), BF16),
                        pltpu.SemaphoreType.DMA, pltpu.SemaphoreType.DMA],
    )
    nl = w1.shape[0]
    return pl.pallas_call(
        functools.partial(_expert_kernel, n_tok=N),
        grid_spec=grid_spec,
        out_shape=jax.ShapeDtypeStruct((N * TOP_K + MOE_BLOCK, D), F32),
        compiler_params=_cparams(("arbitrary",)),
        name="moe_experts",
    )(blk_e, n_used, map3, map3, map3, x, w1, b1.reshape(nl, E, 1, F2), w2, b2.reshape(nl, E, 1, D))


def _combine_ln_kernel(gate_ref, x_ref, y0_ref, y1_ref, y2_ref, y3_ref, g_ref, b_ref, o_ref):
    gate = gate_ref[...]
    f = jnp.zeros(x_ref.shape, F32)
    for j, y_ref in enumerate((y0_ref, y1_ref, y2_ref, y3_ref)):
        f = f + y_ref[...] * gate[:, j:j + 1]
    o_ref[...] = _layer_norm(DN_ALPHA * x_ref[...] + f, g_ref[...], b_ref[...])


def _combine_ln(xt, y4, gates, g, b):
    N, D = xt.shape
    T = MOE_TOK_TILE
    nt = N // T
    yspec = lambda j: pl.BlockSpec((T, D), lambda i: (j * nt + i, 0))
    return pl.pallas_call(
        _combine_ln_kernel,
        grid=(nt,),
        in_specs=[
            pl.BlockSpec((T, TOP_K), lambda i: (i, 0)),
            pl.BlockSpec((T, D), lambda i: (i, 0)),
            yspec(0), yspec(1), yspec(2), yspec(3),
            pl.BlockSpec((1, D), lambda i: (0, 0)),
            pl.BlockSpec((1, D), lambda i: (0, 0)),
        ],
        out_specs=pl.BlockSpec((T, D), lambda i: (i, 0)),
        out_shape=jax.ShapeDtypeStruct((N, D), F32),
        compiler_params=_cparams(("parallel",)),
        name="moe_combine_ln",
    )(gates, xt, y4, y4, y4, y4, g.reshape(1, D), b.reshape(1, D))


def _moe_ln(xt, w_router, b_router, w1, b1, w2, b2, g, b, layer):
    N, D = xt.shape
    idx, gates, rank, counts = _router(xt, w_router, b_router)
    counts = counts.reshape(N_EXPERTS).astype(jnp.int32)
    padded = ((counts + MOE_BLOCK - 1) // MOE_BLOCK) * MOE_BLOCK
    pend = jnp.cumsum(padded)
    pstart = pend - padded
    n_blocks = -(-(N * TOP_K) // MOE_BLOCK) + N_EXPERTS
    hot = idx[:, :, None] == jnp.arange(N_EXPERTS, dtype=jnp.int32)
    dest = jnp.sum(jnp.where(hot, pstart, 0), axis=-1) + rank
    blk_start = jnp.arange(n_blocks, dtype=jnp.int32) * MOE_BLOCK
    blk_e = jnp.minimum(jnp.sum((pend[None, :] <= blk_start[:, None]).astype(jnp.int32), axis=1),
                        N_EXPERTS - 1)
    n_used = (pend[-1:] // MOE_BLOCK).astype(jnp.int32)
    row_map = _rowmap(dest.T.reshape(N * TOP_K), N, n_blocks)
    y4 = _experts(xt, row_map, blk_e, n_used, w1, b1, w2, b2, layer)
    return _combine_ln(xt, y4, gates.T, g, b)


def _dsa_layer(x, w_in, kv_norm, w_uk, w_uv, w_out, g, b):
    Bsz, L, D = x.shape
    qlat, c, ct, qi, ki, wi = _dsa_proj(x, w_in, kv_norm, w_uk)
    o = _dsa_attention(qi, wi, ki, qlat, ct, c, w_uv)
    return _outproj_ln(o.reshape(Bsz * L, -1), w_out, x.reshape(Bsz * L, D), g, b)


def _gla_layer(x, w_in, w_g2, g_bias, norm_g, w_out, g, b):
    Bsz, L, D = x.shape
    xt = x.reshape(Bsz * L, D)
    q, k, v, la, r = _gla_proj(xt, w_in, w_g2, g_bias)
    sh = lambda t: t.reshape(Bsz, L, t.shape[-1])
    o = _gla_core(sh(q), sh(k), sh(v), sh(la), sh(r), norm_g)
    return _outproj_ln(o.reshape(Bsz * L, -1), w_out, xt, g, b)


def kernel(x, a_w_in, a_kv_norm, a_w_uk, a_w_uv, a_w_out, b_w_in, b_w_g2, b_g_bias, b_norm, b_w_out, m_w_router, m_b_router, m_w1, m_b1, m_w2, m_b2, ln1_g, ln1_b, ln2_g, ln2_b):
    Bsz, L, D = x.shape
    for i in range(DEPTH):
        j = i // 2
        if i % 2 == 0:
            xt = _dsa_layer(x, a_w_in[j], a_kv_norm[j], a_w_uk[j], a_w_uv[j], a_w_out[j],
                            ln1_g[i], ln1_b[i])
        else:
            xt = _gla_layer(x, b_w_in[j], b_w_g2[j], b_g_bias[j], b_norm[j], b_w_out[j],
                            ln1_g[i], ln1_b[i])
        xt = _moe_ln(xt, m_w_router[i], m_b_router[i], m_w1, m_b1, m_w2, m_b2,
                     ln2_g[i], ln2_b[i], i)
        x = xt.reshape(Bsz, L, D)
    return x
```

```python
import functools

import jax
import jax.numpy as jnp
from jax import lax
from jax.experimental import pallas as pl
from jax.experimental.pallas import tpu as pltpu

F32 = jnp.float32
BF16 = jnp.bfloat16

DEPTH = 2
DN_ALPHA = (2.0 * DEPTH) ** 0.25
LN_EPS = 1e-5
RMS_EPS = 1e-6

A_HEADS = 16
A_QK_DIM = 64
A_V_DIM = 64
A_LATENT = 256
IDX_HEADS = 8
IDX_DIM = 64
IDX_TOPK_MAX = 256
A_QBLK = 128
A_KBLK = 512
A_ONES = 16
CNT_ROWS = 32
BIS_STEPS = 4
BIS_MAX_STEPS = 36
BIS_FIRST_PIVOT = 0.0625
LOG2E = 1.4426950408889634

B_HEADS = 4
B_GATE_RANK = 16
B_GATE_TAU = 16.0
B_CHUNK = 64
B_SUB = 16
B_TBLK = 512

N_EXPERTS = 32
TOP_K = 4
SWIGLU_LIMIT = 7.0
SWIGLU_ALPHA = 1.702
MOE_BLOCK = 256
ROW_TILE = 512
MOE_TOK_TILE = 256

VMEM_LIMIT = 56 * 1024 * 1024
NEG_BIG = -1e30


def _cparams(sem):
    return pltpu.CompilerParams(dimension_semantics=sem, vmem_limit_bytes=VMEM_LIMIT)


def _dot(a, b):
    return jnp.dot(a, b, preferred_element_type=F32)


def _dot_nt(a, b):
    return lax.dot_general(a, b, (((1,), (1,)), ((), ())), preferred_element_type=F32)


def _dot_tn(a, b):
    return lax.dot_general(a, b, (((0,), (0,)), ((), ())), preferred_element_type=F32)


def _layer_norm(z, g, b):
    mu = jnp.mean(z, axis=-1, keepdims=True)
    zc = z - mu
    var = jnp.mean(zc * zc, axis=-1, keepdims=True)
    return zc * lax.rsqrt(var + LN_EPS) * g + b


def _dsa_proj_kernel(x_ref, wqt_ref, wc_ref, wqit_ref, wki_ref, wwit_ref, kvn_ref, wukt_ref,
                     qlat_ref, c_ref, ct_ref, qi_ref, ki_ref, wi_ref):
    T = x_ref.shape[0]
    QB = A_QBLK
    xb = x_ref[...].astype(BF16)
    qt = _dot_nt(wqt_ref[...], xb)
    for h in range(A_HEADS):
        qh = qt[h * A_QK_DIM:(h + 1) * A_QK_DIM, :].astype(BF16)
        ql = (_dot(wukt_ref[h], qh) * (A_QK_DIM ** -0.5 * LOG2E)).astype(BF16)
        for j in range(T // QB):
            qlat_ref[j, :, h * QB:(h + 1) * QB] = ql[:, j * QB:(j + 1) * QB]
    c = _dot(xb, wc_ref[...])
    c = c * lax.rsqrt(jnp.mean(c * c, axis=-1, keepdims=True) + RMS_EPS) * kvn_ref[...]
    c_ref[...] = c.astype(BF16)
    ct_ref[:A_LATENT, :] = c.T.astype(BF16)
    ct_ref[A_LATENT:, :] = jnp.ones((A_ONES, T), BF16)
    qit = _dot_nt(wqit_ref[...], xb) * (IDX_DIM ** -0.5)
    for h in range(IDX_HEADS):
        qh = qit[h * IDX_DIM:(h + 1) * IDX_DIM, :].astype(BF16)
        for j in range(T // QB):
            qi_ref[j, :, h * QB:(h + 1) * QB] = qh[:, j * QB:(j + 1) * QB]
    ki_ref[...] = _dot(xb, wki_ref[...]).astype(BF16)
    wi_ref[...] = _dot_nt(wwit_ref[...], xb) * (IDX_HEADS ** -0.5)


def _dsa_proj(x, w_in, kv_norm, w_uk):
    Bsz, L, D = x.shape
    T, QB = A_KBLK, A_QBLK
    nkb, nqb, qpt = L // T, L // QB, T // QB
    o1 = A_HEADS * A_QK_DIM
    o2 = o1 + A_LATENT
    o3 = o2 + IDX_HEADS * IDX_DIM
    o4 = o3 + IDX_DIM
    wb = w_in.astype(BF16)
    wqt, wc, wqit, wki, wwit = wb[:, :o1].T, wb[:, o1:o2], wb[:, o2:o3].T, wb[:, o3:o4], wb[:, o4:].T
    wukt = jnp.swapaxes(w_uk, 1, 2).astype(BF16)
    full = lambda shape: pl.BlockSpec(shape, lambda b, t: (0,) * len(shape))
    return pl.pallas_call(
        _dsa_proj_kernel,
        grid=(Bsz, nkb),
        in_specs=[
            pl.BlockSpec((None, T, D), lambda b, t: (b, t, 0)),
            full(wqt.shape), full(wc.shape), full(wqit.shape), full(wki.shape), full(wwit.shape),
            full((1, A_LATENT)), full(wukt.shape),
        ],
        out_specs=[
            pl.BlockSpec((None, qpt, A_LATENT, A_HEADS * QB), lambda b, t: (b, t, 0, 0)),
            pl.BlockSpec((None, None, T, A_LATENT), lambda b, t: (b, t, 0, 0)),
            pl.BlockSpec((None, None, A_LATENT + A_ONES, T), lambda b, t: (b, t, 0, 0)),
            pl.BlockSpec((None, qpt, IDX_DIM, IDX_HEADS * QB), lambda b, t: (b, t, 0, 0)),
            pl.BlockSpec((None, None, T, IDX_DIM), lambda b, t: (b, t, 0, 0)),
            pl.BlockSpec((None, IDX_HEADS, T), lambda b, t: (b, 0, t)),
        ],
        out_shape=[
            jax.ShapeDtypeStruct((Bsz, nqb, A_LATENT, A_HEADS * QB), BF16),
            jax.ShapeDtypeStruct((Bsz, nkb, T, A_LATENT), BF16),
            jax.ShapeDtypeStruct((Bsz, nkb, A_LATENT + A_ONES, T), BF16),
            jax.ShapeDtypeStruct((Bsz, nqb, IDX_DIM, IDX_HEADS * QB), BF16),
            jax.ShapeDtypeStruct((Bsz, nkb, T, IDX_DIM), BF16),
            jax.ShapeDtypeStruct((Bsz, IDX_HEADS, L), F32),
        ],
        compiler_params=_cparams(("parallel", "parallel")),
        name="dsa_proj",
    )(x, wqt, wc, wqit, wki, wwit, kv_norm.reshape(1, A_LATENT), wukt)


def _ordered_int_to_float(o):
    bits = jnp.where(o >= 0, o, o ^ jnp.int32(2 ** 31 - 1))
    return lax.bitcast_convert_type(bits, F32)


def _float_to_ordered_int(x):
    bits = lax.bitcast_convert_type(jnp.asarray(x, F32), jnp.int32)
    return jnp.where(bits >= 0, bits, bits ^ jnp.int32(2 ** 31 - 1))


def _dsa_attn_kernel(qi_ref, wi_ref, ki_ref, qlat_ref, ct_ref, c_ref, wuvt_ref, o_ref,
                     sc_ref, s_ref, m_ref, acc_ref, ot_ref, jcut_ref, *, topk, seq_len):
    QB, KB = A_QBLK, A_KBLK
    qb = pl.program_id(1)
    q0 = qb * QB
    nvb = (q0 + QB + KB - 1) // KB
    qpos = q0 + lax.broadcasted_iota(jnp.int32, (1, QB), 1)
    krow = lax.broadcasted_iota(jnp.int32, (KB, QB), 0)
    kf = float(topk)

    wi = wi_ref[...]
    qi = qi_ref[...]

    def score_body(kb, carry):
        s = _dot(ki_ref[kb], qi)
        sc = jnp.zeros((KB, QB), F32)
        for h in range(IDX_HEADS):
            sc = sc + jnp.maximum(s[:, h * QB:(h + 1) * QB], 0.0) * wi[h:h + 1, :]
        sc_ref[kb] = jnp.where((kb * KB + krow) <= qpos, sc, -jnp.inf)
        return carry

    lax.fori_loop(0, nvb, score_body, 0)

    def count(pred):
        def body(kb, acc):
            hit = jnp.where(pred(sc_ref[kb], kb), 1.0, 0.0)
            return acc + jnp.sum(hit.reshape(KB // CNT_ROWS, CNT_ROWS, QB), axis=0)
        acc = lax.fori_loop(0, nvb, body, jnp.zeros((CNT_ROWS, QB), F32))
        return jnp.sum(acc, axis=0, keepdims=True)

    def row_max():
        def body(kb, acc):
            return jnp.maximum(acc, jnp.max(sc_ref[kb].reshape(KB // CNT_ROWS, CNT_ROWS, QB), axis=0))
        acc = lax.fori_loop(0, nvb, body, jnp.full((CNT_ROWS, QB), -jnp.inf, F32))
        return jnp.max(acc, axis=0, keepdims=True)

    def half_gap(lo, hi):
        return lax.shift_right_logical(hi - lo, jnp.int32(1))

    def bis_step(st, mid):
        lo, hi, thr, done = st
        cand = _ordered_int_to_float(mid)
        cnt = count(lambda blk, kb: blk >= cand)
        ge = cnt >= kf
        hit = (cnt == kf) & (done < 0.5)
        return (jnp.where(ge, mid, lo), jnp.where(ge, hi, mid),
                jnp.where(hit, cand, thr), jnp.where(hit, 1.0, done))

    def bis_cond(st):
        i, lo, hi, _, done = st
        open_ = jnp.where((done < 0.5) & (half_gap(lo, hi) > 0), 1.0, 0.0)
        return (i < BIS_MAX_STEPS) & (jnp.max(open_) > 0.5)

    def bis_body(st):
        i, st = st[0], st[1:]
        for _ in range(BIS_STEPS):
            st = bis_step(st, st[0] + half_gap(st[0], st[1]))
        return (i + BIS_STEPS,) + st

    smax = row_max()
    lo0 = jnp.full((1, QB), _float_to_ordered_int(-jnp.inf), jnp.int32)
    hi0 = _float_to_ordered_int(smax) + 1
    done0 = jnp.where(qpos < topk, 1.0, 0.0)
    st = (lo0, hi0, jnp.full((1, QB), -jnp.inf, F32), done0)
    first = jnp.where(smax > 0.0, _float_to_ordered_int(smax * BIS_FIRST_PIVOT),
                      lo0 + half_gap(lo0, hi0))
    st = bis_step(st, first)
    _, u, _, thr, done = lax.while_loop(bis_cond, bis_body, (jnp.int32(0),) + st)
    thr = jnp.where(done > 0.5, thr, _ordered_int_to_float(u))
    jcut_ref[...] = jnp.full((1, QB), seq_len, jnp.int32)

    @pl.when(jnp.min(done) < 0.5)
    def _():
        need = kf - count(lambda blk, kb: blk > thr)
        nbits = max(1, (seq_len - 1).bit_length())

        def jbody(i, x):
            trial = x | jnp.left_shift(jnp.int32(1), nbits - 1 - i)
            g = count(lambda blk, kb: (blk == thr) & ((kb * KB + krow) < trial))
            return jnp.where(g < need, trial, x)

        x = lax.fori_loop(0, nbits, jbody, jnp.zeros((1, QB), jnp.int32))
        jcut_ref[...] = jnp.where(done > 0.5, seq_len, x)

    jcut = jcut_ref[...]

    m_ref[...] = jnp.full(m_ref.shape, NEG_BIG, F32)
    acc_ref[...] = jnp.zeros(acc_ref.shape, F32)

    pair_cols = [slice(hp * 2 * QB, (hp + 1) * 2 * QB) for hp in range(A_HEADS // 2)]

    def logits(kb, cols):
        return _dot(c_ref[kb], qlat_ref[:, cols])

    def mask_bias(kb):
        blk = sc_ref[kb]
        kpos = kb * KB + krow
        sel = ((blk > thr) | ((blk == thr) & (kpos <= jcut))) & (kpos <= qpos)
        bias = jnp.where(sel, 0.0, NEG_BIG)
        return jnp.concatenate([bias, bias], axis=1)

    def consume(kb, s, cols, bias2):
        s = s + bias2
        m_prev = m_ref[:, cols]
        m_new = jnp.maximum(m_prev, jnp.max(s, axis=0, keepdims=True))
        alpha = jnp.exp2(m_prev - m_new)
        p = jnp.exp2(s - m_new).astype(BF16)
        acc_ref[:, cols] = alpha * acc_ref[:, cols] + _dot(ct_ref[kb], p)
        m_ref[:, cols] = m_new

    for cols in pair_cols:
        s_ref[:, cols] = logits(0, cols)

    def attn_body(kb, carry):
        bias2 = mask_bias(kb)
        for cols in pair_cols:
            s = s_ref[:, cols]
            s_ref[:, cols] = logits(kb + 1, cols)
            consume(kb, s, cols, bias2)
        return carry

    lax.fori_loop(0, nvb - 1, attn_body, 0)
    bias2 = mask_bias(nvb - 1)
    for cols in pair_cols:
        consume(nvb - 1, s_ref[:, cols], cols, bias2)

    for h in range(A_HEADS):
        cols = slice(h * QB, (h + 1) * QB)
        oh = acc_ref[:A_LATENT, cols] / acc_ref[A_LATENT:A_LATENT + 1, cols]
        ot_ref[h * A_V_DIM:(h + 1) * A_V_DIM, :] = _dot(wuvt_ref[h], oh.astype(BF16))
    o_ref[...] = ot_ref[...].T.astype(BF16)


def _dsa_attention(qi, wi, ki, qlat, ct, c, w_uv):
    Bsz, nqb, _, _ = qlat.shape
    QB, KB = A_QBLK, A_KBLK
    L = nqb * QB
    nkb = L // KB
    topk = min(IDX_TOPK_MAX, L // 4)
    wuvt = jnp.swapaxes(w_uv, 1, 2).astype(BF16)
    kern = functools.partial(_dsa_attn_kernel, topk=topk, seq_len=L)
    return pl.pallas_call(
        kern,
        grid=(Bsz, nqb),
        in_specs=[
            pl.BlockSpec((None, None, IDX_DIM, IDX_HEADS * QB), lambda b, q: (b, q, 0, 0)),
            pl.BlockSpec((None, IDX_HEADS, QB), lambda b, q: (b, 0, q)),
            pl.BlockSpec((None, nkb, KB, IDX_DIM), lambda b, q: (b, 0, 0, 0)),
            pl.BlockSpec((None, None, A_LATENT, A_HEADS * QB), lambda b, q: (b, q, 0, 0)),
            pl.BlockSpec((None, nkb, A_LATENT + A_ONES, KB), lambda b, q: (b, 0, 0, 0)),
            pl.BlockSpec((None, nkb, KB, A_LATENT), lambda b, q: (b, 0, 0, 0)),
            pl.BlockSpec((A_HEADS, A_V_DIM, A_LATENT), lambda b, q: (0, 0, 0)),
        ],
        out_specs=pl.BlockSpec((None, QB, A_HEADS * A_V_DIM), lambda b, q: (b, q, 0)),
        out_shape=jax.ShapeDtypeStruct((Bsz, L, A_HEADS * A_V_DIM), BF16),
        scratch_shapes=[
            pltpu.VMEM((nkb, KB, QB), F32),
            pltpu.VMEM((KB, A_HEADS * QB), F32),
            pltpu.VMEM((1, A_HEADS * QB), F32),
            pltpu.VMEM((A_LATENT + A_ONES, A_HEADS * QB), F32),
            pltpu.VMEM((A_HEADS * A_V_DIM, QB), F32),
            pltpu.VMEM((1, QB), jnp.int32),
        ],
        compiler_params=_cparams(("parallel", "parallel")),
        name="dsa_attn",
    )(qi, wi, ki, qlat, ct, c, wuvt)


def _outproj_ln_kernel(o_ref, w_ref, x_ref, g_ref, b_ref, y_ref):
    h = _dot(o_ref[...], w_ref[...])
    y_ref[...] = _layer_norm(DN_ALPHA * x_ref[...] + h, g_ref[...], b_ref[...])


def _outproj_ln(o, w_out, x, g, b):
    N, K = o.shape
    D = x.shape[1]
    T = ROW_TILE
    return pl.pallas_call(
        _outproj_ln_kernel,
        grid=(N // T,),
        in_specs=[
            pl.BlockSpec((T, K), lambda i: (i, 0)),
            pl.BlockSpec((K, D), lambda i: (0, 0)),
            pl.BlockSpec((T, D), lambda i: (i, 0)),
            pl.BlockSpec((1, D), lambda i: (0, 0)),
            pl.BlockSpec((1, D), lambda i: (0, 0)),
        ],
        out_specs=pl.BlockSpec((T, D), lambda i: (i, 0)),
        out_shape=jax.ShapeDtypeStruct((N, D), F32),
        compiler_params=_cparams(("parallel",)),
        name="outproj_ln",
    )(o, w_out.astype(BF16), x, g.reshape(1, D), b.reshape(1, D))


def _gla_proj_kernel(x_ref, wq_ref, wk_ref, wv_ref, wg_ref, wr_ref, wg2_ref, gb_ref,
                     q_ref, k_ref, v_ref, la_ref, r_ref, *, kh):
    xb = x_ref[...].astype(BF16)
    q_ref[...] = _dot(xb, wq_ref[...]) * (kh ** -0.5)
    k_ref[...] = _dot(xb, wk_ref[...])
    v_ref[...] = _dot(xb, wv_ref[...]).astype(BF16)
    r_ref[...] = _dot(xb, wr_ref[...])
    glr = _dot(xb, wg_ref[...])
    z = _dot(glr.astype(BF16), wg2_ref[...]) + gb_ref[...]
    log_sig = jnp.minimum(z, 0.0) - jnp.log(1.0 + jnp.exp(-jnp.abs(z)))
    la_ref[...] = log_sig / B_GATE_TAU


def _gla_proj(x, w_in, w_g2, g_bias):
    N, D = x.shape
    DK = w_g2.shape[1]
    DV = D
    T = ROW_TILE
    wb = w_in.astype(BF16)
    o1, o2, o3, o4 = DK, 2 * DK, 2 * DK + DV, 2 * DK + DV + B_GATE_RANK
    wq, wk, wv, wg, wr = wb[:, :o1], wb[:, o1:o2], wb[:, o2:o3], wb[:, o3:o4], wb[:, o4:]
    full = lambda shape: pl.BlockSpec(shape, lambda i: (0,) * len(shape))
    row = lambda w: pl.BlockSpec((T, w), lambda i: (i, 0))
    kern = functools.partial(_gla_proj_kernel, kh=DK // B_HEADS)
    return pl.pallas_call(
        kern,
        grid=(N // T,),
        in_specs=[row(D), full(wq.shape), full(wk.shape), full(wv.shape), full(wg.shape),
                  full(wr.shape), full(w_g2.shape), full((1, DK))],
        out_specs=[row(DK), row(DK), row(DV), row(DK), row(DV)],
        out_shape=[
            jax.ShapeDtypeStruct((N, DK), F32),
            jax.ShapeDtypeStruct((N, DK), F32),
            jax.ShapeDtypeStruct((N, DV), BF16),
            jax.ShapeDtypeStruct((N, DK), F32),
            jax.ShapeDtypeStruct((N, DV), F32),
        ],
        compiler_params=_cparams(("parallel",)),
        name="gla_proj",
    )(x, wq, wk, wv, wg, wr, w_g2.astype(BF16), g_bias.reshape(1, DK))


def _split3(a):
    hi = a.astype(BF16)
    r1 = a - hi.astype(F32)
    mid = r1.astype(BF16)
    lo = (r1 - mid.astype(F32)).astype(BF16)
    return hi, mid, lo


def _gla_kernel(q_ref, k_ref, v_ref, la_ref, r_ref, ng_ref, o_ref, st_ref, a_ref):
    C, SB = B_CHUNK, B_SUB
    nsub = C // SB
    kh = q_ref.shape[1] // B_HEADS
    vh = v_ref.shape[1] // B_HEADS

    @pl.when(pl.program_id(1) == 0)
    def _():
        st_ref[...] = jnp.zeros(st_ref.shape, F32)

    ri = lax.broadcasted_iota(jnp.int32, (C, C), 0)
    ci = lax.broadcasted_iota(jnp.int32, (C, C), 1)
    tril = jnp.where(ci <= ri, 1.0, 0.0).astype(BF16)
    sub_r = lax.broadcasted_iota(jnp.int32, (SB, 1), 0)
    sub_c = lax.broadcasted_iota(jnp.int32, (1, SB), 1)

    def head_chunk(rows, h):
        kc = slice(h * kh, (h + 1) * kh)
        vc = slice(h * vh, (h + 1) * vh)
        q = q_ref[rows, kc]
        k = k_ref[rows, kc]
        v = v_ref[rows, vc]
        la = la_ref[rows, kc]
        hi, mid, lo = _split3(la)
        b = _dot(tril, hi) + _dot(tril, mid) + _dot(tril, lo)
        st = st_ref[h]
        o = _dot_nt((q * jnp.exp(b)).astype(BF16), st.astype(BF16))

        a_ref[h] = jnp.zeros((C, C), F32)
        for I in range(nsub):
            r0 = I * SB
            bI = b[r0:r0 + SB, :]
            qI = q[r0:r0 + SB, :]
            kI = k[r0:r0 + SB, :]
            if I > 0:
                ref_lvl = b[r0 - 1:r0, :]
                qs = (qI * jnp.exp(bI - ref_lvl)).astype(BF16)
                ks = (k[:r0, :] * jnp.exp(ref_lvl - b[:r0, :])).astype(BF16)
                a_ref[h, r0:r0 + SB, :r0] = _dot_nt(qs, ks)
            diag = jnp.zeros((SB, SB), F32)
            for j in range(SB):
                dlt = jnp.where(sub_r >= j, bI - bI[j:j + 1, :], -jnp.inf)
                col = jnp.sum(qI * kI[j:j + 1, :] * jnp.exp(dlt), axis=1, keepdims=True)
                diag = diag + col * jnp.where(sub_c == j, 1.0, 0.0)
            a_ref[h, r0:r0 + SB, r0:r0 + SB] = diag
        o = o + _dot(a_ref[h].astype(BF16), v)

        b_last = b[C - 1:C, :]
        kd = (k * jnp.exp(b_last - b)).astype(BF16)
        st_ref[h] = st * jnp.exp(b_last) + _dot_tn(v, kd)

        o = o * lax.rsqrt(jnp.mean(o * o, axis=-1, keepdims=True) + RMS_EPS) * ng_ref[:, vc]
        r = r_ref[rows, vc]
        o_ref[rows, vc] = (o * (r / (1.0 + jnp.exp(-r)))).astype(BF16)

    def chunk_body(ch, carry):
        rows = pl.ds(pl.multiple_of(ch * C, C), C)
        for h in range(B_HEADS):
            head_chunk(rows, h)
        return carry

    lax.fori_loop(0, q_ref.shape[0] // C, chunk_body, 0)


def _gla_core(q, k, v, la, r, norm_g):
    Bsz, L, DK = q.shape
    DV = v.shape[2]
    kh, vh = DK // B_HEADS, DV // B_HEADS
    T = B_TBLK
    kspec = pl.BlockSpec((None, T, DK), lambda b, t: (b, t, 0))
    vspec = pl.BlockSpec((None, T, DV), lambda b, t: (b, t, 0))
    return pl.pallas_call(
        _gla_kernel,
        grid=(Bsz, L // T),
        in_specs=[kspec, kspec, vspec, kspec, vspec, pl.BlockSpec((1, DV), lambda b, t: (0, 0))],
        out_specs=vspec,
        out_shape=jax.ShapeDtypeStruct((Bsz, L, DV), BF16),
        scratch_shapes=[pltpu.VMEM((B_HEADS, vh, kh), F32),
                        pltpu.VMEM((B_HEADS, B_CHUNK, B_CHUNK), F32)],
        compiler_params=_cparams(("parallel", "arbitrary")),
        name="gla_core",
    )(q, k, v, la, r, norm_g.reshape(1, DV))


def _router_kernel(x_ref, wt_ref, b_ref, idx_ref, gate_ref, rank_ref, cnt_ref, carry_ref):
    T = x_ref.shape[0]
    E = N_EXPERTS

    @pl.when(pl.program_id(0) == 0)
    def _():
        carry_ref[...] = jnp.zeros(carry_ref.shape, F32)

    x = x_ref[...]
    wt = wt_ref[...]
    xh = x.astype(BF16)
    xl = (x - xh.astype(F32)).astype(BF16)
    wh = wt.astype(BF16)
    wl = (wt - wh.astype(F32)).astype(BF16)
    logits = _dot_nt(wh, xh) + (_dot_nt(wl, xh) + _dot_nt(wh, xl)) + b_ref[...]

    eidx = lax.broadcasted_iota(jnp.int32, (E, T), 0)
    work = logits
    vals, idxs = [], []
    for _ in range(TOP_K):
        mx = jnp.max(work, axis=0, keepdims=True)
        ix = jnp.min(jnp.where(work == mx, eidx, E), axis=0, keepdims=True)
        vals.append(mx)
        idxs.append(ix)
        work = jnp.where(eidx == ix, -jnp.inf, work)
    exps = [jnp.exp(vv - vals[0]) for vv in vals]
    den = exps[0] + exps[1] + exps[2] + exps[3]

    onehot = jnp.zeros((E, T), F32)
    for ix in idxs:
        onehot = onehot + jnp.where(eidx == ix, 1.0, 0.0)
    ri = lax.broadcasted_iota(jnp.int32, (T, T), 0)
    ci = lax.broadcasted_iota(jnp.int32, (T, T), 1)
    before = jnp.where(ri < ci, 1.0, 0.0).astype(BF16)
    prefix = _dot(onehot.astype(BF16), before) + carry_ref[...]
    carry_ref[...] = carry_ref[...] + jnp.sum(onehot, axis=1, keepdims=True)
    cnt_ref[...] = carry_ref[...]

    for j in range(TOP_K):
        idx_ref[j:j + 1, :] = idxs[j]
        gate_ref[j:j + 1, :] = exps[j] / den
        rk = jnp.sum(jnp.where(eidx == idxs[j], prefix, 0.0), axis=0, keepdims=True)
        rank_ref[j:j + 1, :] = rk.astype(jnp.int32)


def _router(xt, w_router, b_router):
    N, D = xt.shape
    T = MOE_TOK_TILE
    tok = pl.BlockSpec((TOP_K, T), lambda i: (0, i))
    return pl.pallas_call(
        _router_kernel,
        grid=(N // T,),
        in_specs=[
            pl.BlockSpec((T, D), lambda i: (i, 0)),
            pl.BlockSpec((N_EXPERTS, D), lambda i: (0, 0)),
            pl.BlockSpec((N_EXPERTS, 1), lambda i: (0, 0)),
        ],
        out_specs=[tok, tok, tok, pl.BlockSpec((N_EXPERTS, 1), lambda i: (0, 0))],
        out_shape=[
            jax.ShapeDtypeStruct((TOP_K, N), jnp.int32),
            jax.ShapeDtypeStruct((TOP_K, N), F32),
            jax.ShapeDtypeStruct((TOP_K, N), jnp.int32),
            jax.ShapeDtypeStruct((N_EXPERTS, 1), F32),
        ],
        scratch_shapes=[pltpu.VMEM((N_EXPERTS, 1), F32)],
        compiler_params=_cparams(("arbitrary",)),
        name="moe_router",
    )(xt, w_router.T, b_router.reshape(N_EXPERTS, 1))


def _rowmap_kernel(dest_ref, init_ref, map_ref, sem, *, n_tok):
    i = pl.program_id(0)
    toks = dest_ref.shape[0] // TOP_K

    @pl.when(i == 0)
    def _():
        cp = pltpu.make_async_copy(init_ref, map_ref, sem)
        cp.start()
        cp.wait()

    t0 = i * toks

    def body(a, c):
        for j in range(TOP_K):
            map_ref[dest_ref[a * TOP_K + j]] = j * n_tok + t0 + a
        return c

    lax.fori_loop(0, toks, body, 0, unroll=8)


def _rowmap(dest_flat, n_tok, n_blocks):
    NK = dest_flat.shape[0]
    tile = min(NK, 8192)
    P = n_blocks * MOE_BLOCK
    init = n_tok * TOP_K + (jnp.arange(P, dtype=jnp.int32) & (MOE_BLOCK - 1))
    return pl.pallas_call(
        functools.partial(_rowmap_kernel, n_tok=n_tok),
        grid=(NK // tile,),
        in_specs=[pl.BlockSpec((tile,), lambda i: (i,), memory_space=pltpu.SMEM),
                  pl.BlockSpec(memory_space=pl.ANY)],
        out_specs=pl.BlockSpec((P,), lambda i: (0,), memory_space=pltpu.SMEM),
        out_shape=jax.ShapeDtypeStruct((P,), jnp.int32),
        scratch_shapes=[pltpu.SemaphoreType.DMA],
        compiler_params=_cparams(("arbitrary",)),
        name="moe_rowmap",
    )(dest_flat, init)


def _expert_kernel(blk_e_ref, nused_ref, dst_prev_ref, dst_cur_ref, tok_cur_ref, tok_next_ref,
                   x_ref, w1_ref, b1_ref, w2_ref, b2_ref, y4_ref,
                   xbuf0, xbuf1, ybuf0, ybuf1, w1b_ref, w2b_ref, gsem, ssem):
    i = pl.program_id(0)
    n_used = nused_ref[0]
    BLK = MOE_BLOCK

    def gather_row(tok_ref, r, xdst):
        return pltpu.make_async_copy(x_ref.at[pl.ds(tok_ref[0, r], 1)], xdst.at[pl.ds(r, 1)], gsem)

    def scatter_row(dst, r, ysrc):
        return pltpu.make_async_copy(ysrc.at[pl.ds(r, 1)], y4_ref.at[pl.ds(dst, 1)], ssem)

    def wait_rows(sem):
        pltpu.make_async_copy(xbuf0, xbuf1, sem).wait()

    @pl.when(i == 0)
    def _():
        ybuf1[...] = jnp.zeros(ybuf1.shape, F32)

        def first(r, c):
            gather_row(tok_cur_ref, r, xbuf0).start()
            return c
        lax.fori_loop(0, BLK, first, 0, unroll=8)

    prev = blk_e_ref[jnp.maximum(i - 1, 0)]
    fresh = (i == 0) | (blk_e_ref[i] != prev)

    @pl.when(fresh & (i < n_used))
    def _():
        w1b_ref[...] = w1_ref[...].astype(BF16)
        w2b_ref[...] = w2_ref[...].astype(BF16)

    def step(xcur, xnext, ycur, yprev):
        wait_rows(gsem)

        @pl.when(i >= 1)
        def _():
            wait_rows(ssem)

        @pl.when(n_used > 0)
        def _():
            for r in range(BLK):
                gather_row(tok_next_ref, r, xnext).start()
                scatter_row(dst_prev_ref[0, r], r, yprev).start()

        F = w2_ref.shape[0]
        xb = xcur[...].astype(BF16)
        h = _dot(xb, w1b_ref[...]) + b1_ref[...]
        g = jnp.minimum(h[:, :F], SWIGLU_LIMIT)
        u = jnp.clip(h[:, F:], -SWIGLU_LIMIT, SWIGLU_LIMIT)
        glu = g / (1.0 + jnp.exp(-SWIGLU_ALPHA * g))
        act = ((u + 1.0) * glu).astype(BF16)
        ycur[...] = _dot(act, w2b_ref[...]) + b2_ref[...]

        @pl.when(i == n_used - 1)
        def _():
            wait_rows(gsem)
            wait_rows(ssem)

            def last(r, c):
                scatter_row(dst_cur_ref[0, r], r, ycur).start()
                return c
            lax.fori_loop(0, BLK, last, 0, unroll=8)
            wait_rows(ssem)

    @pl.when((i < n_used) & (i % 2 == 0))
    def _():
        step(xbuf0, xbuf1, ybuf0, ybuf1)

    @pl.when((i < n_used) & (i % 2 == 1))
    def _():
        step(xbuf1, xbuf0, ybuf1, ybuf0)


def _experts(x, row_map, blk_e, n_used, w1, b1, w2, b2, layer):
    N, D = x.shape
    assert N & (N - 1) == 0, "token count must be a power of two (row map decode)"
    nb = row_map.shape[0] // MOE_BLOCK
    _, E, _, F2 = w1.shape
    F = w2.shape[2]
    spare = N * TOP_K + jnp.arange(MOE_BLOCK, dtype=jnp.int32)
    dst3 = jnp.concatenate([spare, row_map]).reshape(nb + 1, 1, MOE_BLOCK)
    tok3 = (row_map & (N - 1)).reshape(nb, 1, MOE_BLOCK)
    wspec = lambda r, c: pl.BlockSpec((None, None, r, c), lambda i, be, nu: (layer, be[i], 0, 0))
    mspec = lambda off, hi: pl.BlockSpec(
        (None, 1, MOE_BLOCK), lambda i, be, nu: (jnp.minimum(i + off, hi), 0, 0),
        memory_space=pltpu.SMEM)
    grid_spec = pltpu.PrefetchScalarGridSpec(
        num_scalar_prefetch=2,
        grid=(nb,),
        in_specs=[
            mspec(0, nb), mspec(1, nb), mspec(0, nb - 1), mspec(1, nb - 1),
            pl.BlockSpec(memory_space=pl.ANY),
            wspec(D, F2), wspec(1, F2), wspec(F, D), wspec(1, D),
        ],
        out_specs=pl.BlockSpec(memory_space=pl.ANY),
        scratch_shapes=[pltpu.VMEM((MOE_BLOCK, D), F32), pltpu.VMEM((MOE_BLOCK, D), F32),
                        pltpu.VMEM((MOE_BLOCK, D), F32), pltpu.VMEM((MOE_BLOCK, D), F32),
                        pltpu.VMEM((D, F2), BF16), pltpu.VMEM((F, D), BF16),
                        pltpu.SemaphoreType.DMA, pltpu.SemaphoreType.DMA],
    )
    nl = w1.shape[0]
    return pl.pallas_call(
        _expert_kernel,
        grid_spec=grid_spec,
        out_shape=jax.ShapeDtypeStruct((N * TOP_K + MOE_BLOCK, D), F32),
        compiler_params=_cparams(("arbitrary",)),
        name="moe_experts",
    )(blk_e, n_used, dst3, dst3, tok3, tok3, x, w1, b1.reshape(nl, E, 1, F2), w2, b2.reshape(nl, E, 1, D))


def _combine_ln_kernel(gate_ref, x_ref, y0_ref, y1_ref, y2_ref, y3_ref, g_ref, b_ref, o_ref):
    gate = gate_ref[...]
    f = jnp.zeros(x_ref.shape, F32)
    for j, y_ref in enumerate((y0_ref, y1_ref, y2_ref, y3_ref)):
        f = f + y_ref[...] * gate[:, j:j + 1]
    o_ref[...] = _layer_norm(DN_ALPHA * x_ref[...] + f, g_ref[...], b_ref[...])


def _combine_ln(xt, y4, gates, g, b):
    N, D = xt.shape
    T = MOE_TOK_TILE
    nt = N // T
    yspec = lambda j: pl.BlockSpec((T, D), lambda i: (j * nt + i, 0))
    return pl.pallas_call(
        _combine_ln_kernel,
        grid=(nt,),
        in_specs=[
            pl.BlockSpec((T, TOP_K), lambda i: (i, 0)),
            pl.BlockSpec((T, D), lambda i: (i, 0)),
            yspec(0), yspec(1), yspec(2), yspec(3),
            pl.BlockSpec((1, D), lambda i: (0, 0)),
            pl.BlockSpec((1, D), lambda i: (0, 0)),
        ],
        out_specs=pl.BlockSpec((T, D), lambda i: (i, 0)),
        out_shape=jax.ShapeDtypeStruct((N, D), F32),
        compiler_params=_cparams(("parallel",)),
        name="moe_combine_ln",
    )(gates, xt, y4, y4, y4, y4, g.reshape(1, D), b.reshape(1, D))


def _moe_ln(xt, w_router, b_router, w1, b1, w2, b2, g, b, layer):
    N, D = xt.shape
    idx, gates, rank, counts = _router(xt, w_router, b_router)
    counts = counts.reshape(N_EXPERTS).astype(jnp.int32)
    padded = ((counts + MOE_BLOCK - 1) // MOE_BLOCK) * MOE_BLOCK
    pend = jnp.cumsum(padded)
    pstart = pend - padded
    n_blocks = -(-(N * TOP_K) // MOE_BLOCK) + N_EXPERTS
    hot = idx[:, :, None] == jnp.arange(N_EXPERTS, dtype=jnp.int32)
    dest = jnp.sum(jnp.where(hot, pstart, 0), axis=-1) + rank
    blk_start = jnp.arange(n_blocks, dtype=jnp.int32) * MOE_BLOCK
    blk_e = jnp.minimum(jnp.sum((pend[None, :] <= blk_start[:, None]).astype(jnp.int32), axis=1),
                        N_EXPERTS - 1)
    n_used = (pend[-1:] // MOE_BLOCK).astype(jnp.int32)
    row_map = _rowmap(dest.T.reshape(N * TOP_K), N, n_blocks)
    y4 = _experts(xt, row_map, blk_e, n_used, w1, b1, w2, b2, layer)
    return _combine_ln(xt, y4, gates.T, g, b)


def _dsa_layer(x, w_in, kv_norm, w_uk, w_uv, w_out, g, b):
    Bsz, L, D = x.shape
    qlat, c, ct, qi, ki, wi = _dsa_proj(x, w_in, kv_norm, w_uk)
    o = _dsa_attention(qi, wi, ki, qlat, ct, c, w_uv)
    return _outproj_ln(o.reshape(Bsz * L, -1), w_out, x.reshape(Bsz * L, D), g, b)


def _gla_layer(x, w_in, w_g2, g_bias, norm_g, w_out, g, b):
    Bsz, L, D = x.shape
    xt = x.reshape(Bsz * L, D)
    q, k, v, la, r = _gla_proj(xt, w_in, w_g2, g_bias)
    sh = lambda t: t.reshape(Bsz, L, t.shape[-1])
    o = _gla_core(sh(q), sh(k), sh(v), sh(la), sh(r), norm_g)
    return _outproj_ln(o.reshape(Bsz * L, -1), w_out, xt, g, b)


def kernel(x, a_w_in, a_kv_norm, a_w_uk, a_w_uv, a_w_out, b_w_in, b_w_g2, b_g_bias, b_norm, b_w_out, m_w_router, m_b_router, m_w1, m_b1, m_w2, m_b2, ln1_g, ln1_b, ln2_g, ln2_b):
    Bsz, L, D = x.shape
    for i in range(DEPTH):
        j = i // 2
        if i % 2 == 0:
            xt = _dsa_layer(x, a_w_in[j], a_kv_norm[j], a_w_uk[j], a_w_uv[j], a_w_out[j],
                            ln1_g[i], ln1_b[i])
        else:
            xt = _gla_layer(x, b_w_in[j], b_w_g2[j], b_g_bias[j], b_norm[j], b_w_out[j],
                            ln1_g[i], ln1_b[i])
        xt = _moe_ln(xt, m_w_router[i], m_b_router[i], m_w1, m_b1, m_w2, m_b2,
                     ln2_g[i], ln2_b[i], i)
        x = xt.reshape(Bsz, L, D)
    return x
```

```python
import functools

import jax
import jax.numpy as jnp
from jax import lax
from jax.experimental import pallas as pl
from jax.experimental.pallas import tpu as pltpu

F32 = jnp.float32
BF16 = jnp.bfloat16

DEPTH = 2
DN_ALPHA = (2.0 * DEPTH) ** 0.25
LN_EPS = 1e-5
RMS_EPS = 1e-6

A_HEADS = 16
A_QK_DIM = 64
A_V_DIM = 64
A_LATENT = 256
IDX_HEADS = 8
IDX_DIM = 64
IDX_TOPK_MAX = 256
A_QBLK = 128
A_KBLK = 512
A_ONES = 16
CNT_ROWS = 32
BIS_STEPS = 4
BIS_MAX_STEPS = 36
BIS_FIRST_PIVOT = 0.0625
LOG2E = 1.4426950408889634

B_HEADS = 4
B_GATE_RANK = 16
B_GATE_TAU = 16.0
B_CHUNK = 64
B_SUB = 16
B_TBLK = 512

N_EXPERTS = 32
TOP_K = 4
SWIGLU_LIMIT = 7.0
SWIGLU_ALPHA = 1.702
MOE_BLOCK = 256
ROW_TILE = 512
MOE_TOK_TILE = 256

VMEM_LIMIT = 56 * 1024 * 1024
NEG_BIG = -1e30


def _cparams(sem):
    return pltpu.CompilerParams(dimension_semantics=sem, vmem_limit_bytes=VMEM_LIMIT)


def _dot(a, b):
    return jnp.dot(a, b, preferred_element_type=F32)


def _dot_nt(a, b):
    return lax.dot_general(a, b, (((1,), (1,)), ((), ())), preferred_element_type=F32)


def _dot_tn(a, b):
    return lax.dot_general(a, b, (((0,), (0,)), ((), ())), preferred_element_type=F32)


def _layer_norm(z, g, b):
    mu = jnp.mean(z, axis=-1, keepdims=True)
    zc = z - mu
    var = jnp.mean(zc * zc, axis=-1, keepdims=True)
    return zc * lax.rsqrt(var + LN_EPS) * g + b


def _dsa_proj_kernel(x_ref, wqt_ref, wc_ref, wqit_ref, wki_ref, wwit_ref, kvn_ref, wukt_ref,
                     qlat_ref, c_ref, ct_ref, qi_ref, ki_ref, wi_ref):
    T = x_ref.shape[0]
    QB = A_QBLK
    xb = x_ref[...].astype(BF16)
    qt = _dot_nt(wqt_ref[...], xb)
    for h in range(A_HEADS):
        qh = qt[h * A_QK_DIM:(h + 1) * A_QK_DIM, :].astype(BF16)
        ql = (_dot(wukt_ref[h], qh) * (A_QK_DIM ** -0.5 * LOG2E)).astype(BF16)
        for j in range(T // QB):
            qlat_ref[j, :, h * QB:(h + 1) * QB] = ql[:, j * QB:(j + 1) * QB]
    c = _dot(xb, wc_ref[...])
    c = c * lax.rsqrt(jnp.mean(c * c, axis=-1, keepdims=True) + RMS_EPS) * kvn_ref[...]
    c_ref[...] = c.astype(BF16)
    ct_ref[:A_LATENT, :] = c.T.astype(BF16)
    ct_ref[A_LATENT:, :] = jnp.ones((A_ONES, T), BF16)
    qit = _dot_nt(wqit_ref[...], xb) * (IDX_DIM ** -0.5)
    for h in range(IDX_HEADS):
        qh = qit[h * IDX_DIM:(h + 1) * IDX_DIM, :].astype(BF16)
        for j in range(T // QB):
            qi_ref[j, :, h * QB:(h + 1) * QB] = qh[:, j * QB:(j + 1) * QB]
    ki_ref[...] = _dot(xb, wki_ref[...]).astype(BF16)
    wi_ref[...] = _dot_nt(wwit_ref[...], xb) * (IDX_HEADS ** -0.5)


def _dsa_proj(x, w_in, kv_norm, w_uk):
    Bsz, L, D = x.shape
    T, QB = A_KBLK, A_QBLK
    nkb, nqb, qpt = L // T, L // QB, T // QB
    o1 = A_HEADS * A_QK_DIM
    o2 = o1 + A_LATENT
    o3 = o2 + IDX_HEADS * IDX_DIM
    o4 = o3 + IDX_DIM
    wb = w_in.astype(BF16)
    wqt, wc, wqit, wki, wwit = wb[:, :o1].T, wb[:, o1:o2], wb[:, o2:o3].T, wb[:, o3:o4], wb[:, o4:].T
    wukt = jnp.swapaxes(w_uk, 1, 2).astype(BF16)
    full = lambda shape: pl.BlockSpec(shape, lambda b, t: (0,) * len(shape))
    return pl.pallas_call(
        _dsa_proj_kernel,
        grid=(Bsz, nkb),
        in_specs=[
            pl.BlockSpec((None, T, D), lambda b, t: (b, t, 0)),
            full(wqt.shape), full(wc.shape), full(wqit.shape), full(wki.shape), full(wwit.shape),
            full((1, A_LATENT)), full(wukt.shape),
        ],
        out_specs=[
            pl.BlockSpec((None, qpt, A_LATENT, A_HEADS * QB), lambda b, t: (b, t, 0, 0)),
            pl.BlockSpec((None, None, T, A_LATENT), lambda b, t: (b, t, 0, 0)),
            pl.BlockSpec((None, None, A_LATENT + A_ONES, T), lambda b, t: (b, t, 0, 0)),
            pl.BlockSpec((None, qpt, IDX_DIM, IDX_HEADS * QB), lambda b, t: (b, t, 0, 0)),
            pl.BlockSpec((None, None, T, IDX_DIM), lambda b, t: (b, t, 0, 0)),
            pl.BlockSpec((None, IDX_HEADS, T), lambda b, t: (b, 0, t)),
        ],
        out_shape=[
            jax.ShapeDtypeStruct((Bsz, nqb, A_LATENT, A_HEADS * QB), BF16),
            jax.ShapeDtypeStruct((Bsz, nkb, T, A_LATENT), BF16),
            jax.ShapeDtypeStruct((Bsz, nkb, A_LATENT + A_ONES, T), BF16),
            jax.ShapeDtypeStruct((Bsz, nqb, IDX_DIM, IDX_HEADS * QB), BF16),
            jax.ShapeDtypeStruct((Bsz, nkb, T, IDX_DIM), BF16),
            jax.ShapeDtypeStruct((Bsz, IDX_HEADS, L), F32),
        ],
        compiler_params=_cparams(("parallel", "parallel")),
        name="dsa_proj",
    )(x, wqt, wc, wqit, wki, wwit, kv_norm.reshape(1, A_LATENT), wukt)


def _ordered_int_to_float(o):
    bits = jnp.where(o >= 0, o, o ^ jnp.int32(2 ** 31 - 1))
    return lax.bitcast_convert_type(bits, F32)


def _float_to_ordered_int(x):
    bits = lax.bitcast_convert_type(jnp.asarray(x, F32), jnp.int32)
    return jnp.where(bits >= 0, bits, bits ^ jnp.int32(2 ** 31 - 1))


def _dsa_attn_kernel(qi_ref, wi_ref, ki_ref, qlat_ref, ct_ref, c_ref, wuvt_ref, o_ref,
                     sc_ref, s_ref, m_ref, acc_ref, ot_ref, jcut_ref, *, topk, seq_len):
    QB, KB = A_QBLK, A_KBLK
    qb = pl.program_id(1)
    q0 = qb * QB
    nvb = (q0 + QB + KB - 1) // KB
    qpos = q0 + lax.broadcasted_iota(jnp.int32, (1, QB), 1)
    krow = lax.broadcasted_iota(jnp.int32, (KB, QB), 0)
    kf = float(topk)

    wi = wi_ref[...]
    qi = qi_ref[...]

    def score_body(kb, carry):
        s = _dot(ki_ref[kb], qi)
        sc = jnp.zeros((KB, QB), F32)
        for h in range(IDX_HEADS):
            sc = sc + jnp.maximum(s[:, h * QB:(h + 1) * QB], 0.0) * wi[h:h + 1, :]
        sc_ref[kb] = jnp.where((kb * KB + krow) <= qpos, sc, -jnp.inf)
        return carry

    lax.fori_loop(0, nvb, score_body, 0)

    def count(pred):
        def body(kb, acc):
            hit = jnp.where(pred(sc_ref[kb], kb), 1.0, 0.0)
            return acc + jnp.sum(hit.reshape(KB // CNT_ROWS, CNT_ROWS, QB), axis=0)
        acc = lax.fori_loop(0, nvb, body, jnp.zeros((CNT_ROWS, QB), F32))
        return jnp.sum(acc, axis=0, keepdims=True)

    def row_max():
        def body(kb, acc):
            return jnp.maximum(acc, jnp.max(sc_ref[kb].reshape(KB // CNT_ROWS, CNT_ROWS, QB), axis=0))
        acc = lax.fori_loop(0, nvb, body, jnp.full((CNT_ROWS, QB), -jnp.inf, F32))
        return jnp.max(acc, axis=0, keepdims=True)

    def half_gap(lo, hi):
        return lax.shift_right_logical(hi - lo, jnp.int32(1))

    def bis_step(st, mid):
        lo, hi, thr, done = st
        cand = _ordered_int_to_float(mid)
        cnt = count(lambda blk, kb: blk >= cand)
        ge = cnt >= kf
        hit = (cnt == kf) & (done < 0.5)
        return (jnp.where(ge, mid, lo), jnp.where(ge, hi, mid),
                jnp.where(hit, cand, thr), jnp.where(hit, 1.0, done))

    def bis_cond(st):
        i, lo, hi, _, done = st
        open_ = jnp.where((done < 0.5) & (half_gap(lo, hi) > 0), 1.0, 0.0)
        return (i < BIS_MAX_STEPS) & (jnp.max(open_) > 0.5)

    def bis_body(st):
        i, st = st[0], st[1:]
        for _ in range(BIS_STEPS):
            st = bis_step(st, st[0] + half_gap(st[0], st[1]))
        return (i + BIS_STEPS,) + st

    smax = row_max()
    lo0 = jnp.full((1, QB), _float_to_ordered_int(-jnp.inf), jnp.int32)
    hi0 = _float_to_ordered_int(smax) + 1
    done0 = jnp.where(qpos < topk, 1.0, 0.0)
    st = (lo0, hi0, jnp.full((1, QB), -jnp.inf, F32), done0)
    first = jnp.where(smax > 0.0, _float_to_ordered_int(smax * BIS_FIRST_PIVOT),
                      lo0 + half_gap(lo0, hi0))
    st = bis_step(st, first)
    _, u, _, thr, done = lax.while_loop(bis_cond, bis_body, (jnp.int32(0),) + st)
    thr = jnp.where(done > 0.5, thr, _ordered_int_to_float(u))
    jcut_ref[...] = jnp.full((1, QB), seq_len, jnp.int32)

    @pl.when(jnp.min(done) < 0.5)
    def _():
        need = kf - count(lambda blk, kb: blk > thr)
        nbits = max(1, (seq_len - 1).bit_length())

        def jbody(i, x):
            trial = x | jnp.left_shift(jnp.int32(1), nbits - 1 - i)
            g = count(lambda blk, kb: (blk == thr) & ((kb * KB + krow) < trial))
            return jnp.where(g < need, trial, x)

        x = lax.fori_loop(0, nbits, jbody, jnp.zeros((1, QB), jnp.int32))
        jcut_ref[...] = jnp.where(done > 0.5, seq_len, x)

    jcut = jcut_ref[...]

    m_ref[...] = jnp.full(m_ref.shape, NEG_BIG, F32)
    acc_ref[...] = jnp.zeros(acc_ref.shape, F32)

    pair_cols = [slice(hp * 2 * QB, (hp + 1) * 2 * QB) for hp in range(A_HEADS // 2)]

    def logits(kb, cols):
        return _dot(c_ref[kb], qlat_ref[:, cols])

    def mask_bias(kb):
        blk = sc_ref[kb]
        kpos = kb * KB + krow
        sel = ((blk > thr) | ((blk == thr) & (kpos <= jcut))) & (kpos <= qpos)
        bias = jnp.where(sel, 0.0, NEG_BIG)
        return jnp.concatenate([bias, bias], axis=1)

    def consume(kb, s, cols, bias2):
        s = s + bias2
        m_prev = m_ref[:, cols]
        m_new = jnp.maximum(m_prev, jnp.max(s, axis=0, keepdims=True))
        alpha = jnp.exp2(m_prev - m_new)
        p = jnp.exp2(s - m_new).astype(BF16)
        acc_ref[:, cols] = alpha * acc_ref[:, cols] + _dot(ct_ref[kb], p)
        m_ref[:, cols] = m_new

    for cols in pair_cols:
        s_ref[:, cols] = logits(0, cols)

    def attn_body(kb, carry):
        bias2 = mask_bias(kb)
        for cols in pair_cols:
            s = s_ref[:, cols]
            s_ref[:, cols] = logits(kb + 1, cols)
            consume(kb, s, cols, bias2)
        return carry

    lax.fori_loop(0, nvb - 1, attn_body, 0)
    bias2 = mask_bias(nvb - 1)
    for cols in pair_cols:
        consume(nvb - 1, s_ref[:, cols], cols, bias2)

    for h in range(A_HEADS):
        cols = slice(h * QB, (h + 1) * QB)
        oh = acc_ref[:A_LATENT, cols] / acc_ref[A_LATENT:A_LATENT + 1, cols]
        ot_ref[h * A_V_DIM:(h + 1) * A_V_DIM, :] = _dot(wuvt_ref[h], oh.astype(BF16))
    o_ref[...] = ot_ref[...].T.astype(BF16)


def _dsa_attention(qi, wi, ki, qlat, ct, c, w_uv):
    Bsz, nqb, _, _ = qlat.shape
    QB, KB = A_QBLK, A_KBLK
    L = nqb * QB
    nkb = L // KB
    topk = min(IDX_TOPK_MAX, L // 4)
    wuvt = jnp.swapaxes(w_uv, 1, 2).astype(BF16)
    kern = functools.partial(_dsa_attn_kernel, topk=topk, seq_len=L)
    return pl.pallas_call(
        kern,
        grid=(Bsz, nqb),
        in_specs=[
            pl.BlockSpec((None, None, IDX_DIM, IDX_HEADS * QB), lambda b, q: (b, q, 0, 0)),
            pl.BlockSpec((None, IDX_HEADS, QB), lambda b, q: (b, 0, q)),
            pl.BlockSpec((None, nkb, KB, IDX_DIM), lambda b, q: (b, 0, 0, 0)),
            pl.BlockSpec((None, None, A_LATENT, A_HEADS * QB), lambda b, q: (b, q, 0, 0)),
            pl.BlockSpec((None, nkb, A_LATENT + A_ONES, KB), lambda b, q: (b, 0, 0, 0)),
            pl.BlockSpec((None, nkb, KB, A_LATENT), lambda b, q: (b, 0, 0, 0)),
            pl.BlockSpec((A_HEADS, A_V_DIM, A_LATENT), lambda b, q: (0, 0, 0)),
        ],
        out_specs=pl.BlockSpec((None, QB, A_HEADS * A_V_DIM), lambda b, q: (b, q, 0)),
        out_shape=jax.ShapeDtypeStruct((Bsz, L, A_HEADS * A_V_DIM), BF16),
        scratch_shapes=[
            pltpu.VMEM((nkb, KB, QB), F32),
            pltpu.VMEM((KB, A_HEADS * QB), F32),
            pltpu.VMEM((1, A_HEADS * QB), F32),
            pltpu.VMEM((A_LATENT + A_ONES, A_HEADS * QB), F32),
            pltpu.VMEM((A_HEADS * A_V_DIM, QB), F32),
            pltpu.VMEM((1, QB), jnp.int32),
        ],
        compiler_params=_cparams(("parallel", "parallel")),
        name="dsa_attn",
    )(qi, wi, ki, qlat, ct, c, wuvt)


def _outproj_ln_kernel(o_ref, w_ref, x_ref, g_ref, b_ref, wrt_ref, br_ref,
                       y_ref, idx_ref, gate_ref, rank_ref, cnt_ref, carry_ref):
    h = _dot(o_ref[...], w_ref[...])
    y = _layer_norm(DN_ALPHA * x_ref[...] + h, g_ref[...], b_ref[...])
    y_ref[...] = y
    _route(y, wrt_ref, br_ref, idx_ref, gate_ref, rank_ref, cnt_ref, carry_ref)


def _outproj_ln(o, w_out, x, g, b, w_router, b_router):
    N, K = o.shape
    D = x.shape[1]
    T = ROW_TILE
    tok = pl.BlockSpec((TOP_K, T), lambda i: (0, i))
    outs = pl.pallas_call(
        _outproj_ln_kernel,
        grid=(N // T,),
        in_specs=[
            pl.BlockSpec((T, K), lambda i: (i, 0)),
            pl.BlockSpec((K, D), lambda i: (0, 0)),
            pl.BlockSpec((T, D), lambda i: (i, 0)),
            pl.BlockSpec((1, D), lambda i: (0, 0)),
            pl.BlockSpec((1, D), lambda i: (0, 0)),
            pl.BlockSpec((N_EXPERTS, D), lambda i: (0, 0)),
            pl.BlockSpec((N_EXPERTS, 1), lambda i: (0, 0)),
        ],
        out_specs=[pl.BlockSpec((T, D), lambda i: (i, 0)), tok, tok, tok,
                   pl.BlockSpec((N_EXPERTS, 1), lambda i: (0, 0))],
        out_shape=[
            jax.ShapeDtypeStruct((N, D), F32),
            jax.ShapeDtypeStruct((TOP_K, N), jnp.int32),
            jax.ShapeDtypeStruct((TOP_K, N), F32),
            jax.ShapeDtypeStruct((TOP_K, N), jnp.int32),
            jax.ShapeDtypeStruct((N_EXPERTS, 1), F32),
        ],
        scratch_shapes=[pltpu.VMEM((N_EXPERTS, 1), F32)],
        compiler_params=_cparams(("arbitrary",)),
        name="outproj_ln_route",
    )(o, w_out.astype(BF16), x, g.reshape(1, D), b.reshape(1, D),
      w_router.T, b_router.reshape(N_EXPERTS, 1))
    return outs[0], tuple(outs[1:])


def _gla_project(x, wq_ref, wk_ref, wv_ref, wg_ref, wr_ref, wg2_ref, gb_ref,
                 q_ref, k_ref, v_ref, la_ref, r_ref, kh):
    xb = x.astype(BF16)
    q_ref[...] = _dot(xb, wq_ref[...]) * (kh ** -0.5)
    k_ref[...] = _dot(xb, wk_ref[...])
    v_ref[...] = _dot(xb, wv_ref[...]).astype(BF16)
    r_ref[...] = _dot(xb, wr_ref[...])
    glr = _dot(xb, wg_ref[...])
    z = _dot(glr.astype(BF16), wg2_ref[...]) + gb_ref[...]
    log_sig = jnp.minimum(z, 0.0) - jnp.log(1.0 + jnp.exp(-jnp.abs(z)))
    la_ref[...] = log_sig / B_GATE_TAU


def _gla_proj_operands(N, D, T, w_in, w_g2, g_bias):
    DK = w_g2.shape[1]
    DV = D
    wb = w_in.astype(BF16)
    o1, o2, o3, o4 = DK, 2 * DK, 2 * DK + DV, 2 * DK + DV + B_GATE_RANK
    ws = [wb[:, :o1], wb[:, o1:o2], wb[:, o2:o3], wb[:, o3:o4], wb[:, o4:],
          w_g2.astype(BF16), g_bias.reshape(1, DK)]
    full = lambda shape: pl.BlockSpec(shape, lambda i: (0,) * len(shape))
    row = lambda w: pl.BlockSpec((T, w), lambda i: (i, 0))
    out_specs = [row(DK), row(DK), row(DV), row(DK), row(DV)]
    out_shape = [
        jax.ShapeDtypeStruct((N, DK), F32),
        jax.ShapeDtypeStruct((N, DK), F32),
        jax.ShapeDtypeStruct((N, DV), BF16),
        jax.ShapeDtypeStruct((N, DK), F32),
        jax.ShapeDtypeStruct((N, DV), F32),
    ]
    return ws, [full(w.shape) for w in ws], out_specs, out_shape, DK // B_HEADS


def _split3(a):
    hi = a.astype(BF16)
    r1 = a - hi.astype(F32)
    mid = r1.astype(BF16)
    lo = (r1 - mid.astype(F32)).astype(BF16)
    return hi, mid, lo


def _gla_kernel(q_ref, k_ref, v_ref, la_ref, r_ref, ng_ref, o_ref, st_ref, a_ref):
    C, SB = B_CHUNK, B_SUB
    nsub = C // SB
    kh = q_ref.shape[1] // B_HEADS
    vh = v_ref.shape[1] // B_HEADS

    @pl.when(pl.program_id(1) == 0)
    def _():
        st_ref[...] = jnp.zeros(st_ref.shape, F32)

    ri = lax.broadcasted_iota(jnp.int32, (C, C), 0)
    ci = lax.broadcasted_iota(jnp.int32, (C, C), 1)
    tril = jnp.where(ci <= ri, 1.0, 0.0).astype(BF16)
    sub_r = lax.broadcasted_iota(jnp.int32, (SB, 1), 0)
    sub_c = lax.broadcasted_iota(jnp.int32, (1, SB), 1)

    def head_chunk(rows, h):
        kc = slice(h * kh, (h + 1) * kh)
        vc = slice(h * vh, (h + 1) * vh)
        q = q_ref[rows, kc]
        k = k_ref[rows, kc]
        v = v_ref[rows, vc]
        la = la_ref[rows, kc]
        hi, mid, lo = _split3(la)
        b = _dot(tril, hi) + _dot(tril, mid) + _dot(tril, lo)
        st = st_ref[h]
        o = _dot_nt((q * jnp.exp(b)).astype(BF16), st.astype(BF16))

        a_ref[h] = jnp.zeros((C, C), F32)
        for I in range(nsub):
            r0 = I * SB
            bI = b[r0:r0 + SB, :]
            qI = q[r0:r0 + SB, :]
            kI = k[r0:r0 + SB, :]
            if I > 0:
                ref_lvl = b[r0 - 1:r0, :]
                qs = (qI * jnp.exp(bI - ref_lvl)).astype(BF16)
                ks = (k[:r0, :] * jnp.exp(ref_lvl - b[:r0, :])).astype(BF16)
                a_ref[h, r0:r0 + SB, :r0] = _dot_nt(qs, ks)
            diag = jnp.zeros((SB, SB), F32)
            for j in range(SB):
                dlt = jnp.where(sub_r >= j, bI - bI[j:j + 1, :], -jnp.inf)
                col = jnp.sum(qI * kI[j:j + 1, :] * jnp.exp(dlt), axis=1, keepdims=True)
                diag = diag + col * jnp.where(sub_c == j, 1.0, 0.0)
            a_ref[h, r0:r0 + SB, r0:r0 + SB] = diag
        o = o + _dot(a_ref[h].astype(BF16), v)

        b_last = b[C - 1:C, :]
        kd = (k * jnp.exp(b_last - b)).astype(BF16)
        st_ref[h] = st * jnp.exp(b_last) + _dot_tn(v, kd)

        o = o * lax.rsqrt(jnp.mean(o * o, axis=-1, keepdims=True) + RMS_EPS) * ng_ref[:, vc]
        r = r_ref[rows, vc]
        o_ref[rows, vc] = (o * (r / (1.0 + jnp.exp(-r)))).astype(BF16)

    def chunk_body(ch, carry):
        rows = pl.ds(pl.multiple_of(ch * C, C), C)
        for h in range(B_HEADS):
            head_chunk(rows, h)
        return carry

    lax.fori_loop(0, q_ref.shape[0] // C, chunk_body, 0)


def _gla_core(q, k, v, la, r, norm_g):
    Bsz, L, DK = q.shape
    DV = v.shape[2]
    kh, vh = DK // B_HEADS, DV // B_HEADS
    T = B_TBLK
    kspec = pl.BlockSpec((None, T, DK), lambda b, t: (b, t, 0))
    vspec = pl.BlockSpec((None, T, DV), lambda b, t: (b, t, 0))
    return pl.pallas_call(
        _gla_kernel,
        grid=(Bsz, L // T),
        in_specs=[kspec, kspec, vspec, kspec, vspec, pl.BlockSpec((1, DV), lambda b, t: (0, 0))],
        out_specs=vspec,
        out_shape=jax.ShapeDtypeStruct((Bsz, L, DV), BF16),
        scratch_shapes=[pltpu.VMEM((B_HEADS, vh, kh), F32),
                        pltpu.VMEM((B_HEADS, B_CHUNK, B_CHUNK), F32)],
        compiler_params=_cparams(("parallel", "arbitrary")),
        name="gla_core",
    )(q, k, v, la, r, norm_g.reshape(1, DV))


def _route(x, wt_ref, b_ref, idx_ref, gate_ref, rank_ref, cnt_ref, carry_ref):
    T = x.shape[0]
    E = N_EXPERTS

    @pl.when(pl.program_id(0) == 0)
    def _():
        carry_ref[...] = jnp.zeros(carry_ref.shape, F32)

    wt = wt_ref[...]
    xh = x.astype(BF16)
    xl = (x - xh.astype(F32)).astype(BF16)
    wh = wt.astype(BF16)
    wl = (wt - wh.astype(F32)).astype(BF16)
    logits = _dot_nt(wh, xh) + (_dot_nt(wl, xh) + _dot_nt(wh, xl)) + b_ref[...]

    eidx = lax.broadcasted_iota(jnp.int32, (E, T), 0)
    work = logits
    vals, idxs = [], []
    for _ in range(TOP_K):
        mx = jnp.max(work, axis=0, keepdims=True)
        ix = jnp.min(jnp.where(work == mx, eidx, E), axis=0, keepdims=True)
        vals.append(mx)
        idxs.append(ix)
        work = jnp.where(eidx == ix, -jnp.inf, work)
    exps = [jnp.exp(vv - vals[0]) for vv in vals]
    den = exps[0] + exps[1] + exps[2] + exps[3]

    onehot = jnp.zeros((E, T), F32)
    for ix in idxs:
        onehot = onehot + jnp.where(eidx == ix, 1.0, 0.0)
    ri = lax.broadcasted_iota(jnp.int32, (T, T), 0)
    ci = lax.broadcasted_iota(jnp.int32, (T, T), 1)
    before = jnp.where(ri < ci, 1.0, 0.0).astype(BF16)
    prefix = _dot(onehot.astype(BF16), before) + carry_ref[...]
    carry_ref[...] = carry_ref[...] + jnp.sum(onehot, axis=1, keepdims=True)
    cnt_ref[...] = carry_ref[...]

    for j in range(TOP_K):
        idx_ref[j:j + 1, :] = idxs[j]
        gate_ref[j:j + 1, :] = exps[j] / den
        rk = jnp.sum(jnp.where(eidx == idxs[j], prefix, 0.0), axis=0, keepdims=True)
        rank_ref[j:j + 1, :] = rk.astype(jnp.int32)


def _rowmap_kernel(dest_ref, init_ref, map_ref, sem, *, n_tok):
    i = pl.program_id(0)
    toks = dest_ref.shape[0] // TOP_K

    @pl.when(i == 0)
    def _():
        cp = pltpu.make_async_copy(init_ref, map_ref, sem)
        cp.start()
        cp.wait()

    t0 = i * toks

    def body(a, c):
        for j in range(TOP_K):
            map_ref[dest_ref[a * TOP_K + j]] = j * n_tok + t0 + a
        return c

    lax.fori_loop(0, toks, body, 0, unroll=8)


def _rowmap(dest_flat, n_tok, n_blocks):
    NK = dest_flat.shape[0]
    tile = min(NK, 8192)
    P = n_blocks * MOE_BLOCK
    init = n_tok * TOP_K + (jnp.arange(P, dtype=jnp.int32) & (MOE_BLOCK - 1))
    return pl.pallas_call(
        functools.partial(_rowmap_kernel, n_tok=n_tok),
        grid=(NK // tile,),
        in_specs=[pl.BlockSpec((tile,), lambda i: (i,), memory_space=pltpu.SMEM),
                  pl.BlockSpec(memory_space=pl.ANY)],
        out_specs=pl.BlockSpec((P,), lambda i: (0,), memory_space=pltpu.SMEM),
        out_shape=jax.ShapeDtypeStruct((P,), jnp.int32),
        scratch_shapes=[pltpu.SemaphoreType.DMA],
        compiler_params=_cparams(("arbitrary",)),
        name="moe_rowmap",
    )(dest_flat, init)


def _expert_kernel(blk_e_ref, nused_ref, dst_prev_ref, dst_cur_ref, tok_cur_ref, tok_next_ref,
                   x_ref, w1_ref, b1_ref, w2_ref, b2_ref, y4_ref,
                   xbuf0, xbuf1, ybuf0, ybuf1, w1b_ref, w2b_ref, gsem, ssem):
    i = pl.program_id(0)
    n_used = nused_ref[0]
    BLK = MOE_BLOCK

    def gather_row(tok_ref, r, xdst):
        return pltpu.make_async_copy(x_ref.at[pl.ds(tok_ref[0, r], 1)], xdst.at[pl.ds(r, 1)], gsem)

    def scatter_row(dst, r, ysrc):
        return pltpu.make_async_copy(ysrc.at[pl.ds(r, 1)], y4_ref.at[pl.ds(dst, 1)], ssem)

    def wait_rows(sem):
        pltpu.make_async_copy(xbuf0, xbuf1, sem).wait()

    @pl.when(i == 0)
    def _():
        ybuf1[...] = jnp.zeros(ybuf1.shape, F32)

        def first(r, c):
            gather_row(tok_cur_ref, r, xbuf0).start()
            return c
        lax.fori_loop(0, BLK, first, 0, unroll=8)

    prev = blk_e_ref[jnp.maximum(i - 1, 0)]
    fresh = (i == 0) | (blk_e_ref[i] != prev)

    @pl.when(fresh & (i < n_used))
    def _():
        w1b_ref[...] = w1_ref[...].astype(BF16)
        w2b_ref[...] = w2_ref[...].astype(BF16)

    def step(xcur, xnext, ycur, yprev):
        wait_rows(gsem)

        @pl.when(i >= 1)
        def _():
            wait_rows(ssem)

        @pl.when(n_used > 0)
        def _():
            for r in range(BLK):
                gather_row(tok_next_ref, r, xnext).start()
                scatter_row(dst_prev_ref[0, r], r, yprev).start()

        F = w2_ref.shape[0]
        xb = xcur[...].astype(BF16)
        h = _dot(xb, w1b_ref[...]) + b1_ref[...]
        g = jnp.minimum(h[:, :F], SWIGLU_LIMIT)
        u = jnp.clip(h[:, F:], -SWIGLU_LIMIT, SWIGLU_LIMIT)
        glu = g / (1.0 + jnp.exp(-SWIGLU_ALPHA * g))
        act = ((u + 1.0) * glu).astype(BF16)
        ycur[...] = _dot(act, w2b_ref[...]) + b2_ref[...]

        @pl.when(i == n_used - 1)
        def _():
            wait_rows(gsem)
            wait_rows(ssem)

            def last(r, c):
                scatter_row(dst_cur_ref[0, r], r, ycur).start()
                return c
            lax.fori_loop(0, BLK, last, 0, unroll=8)
            wait_rows(ssem)

    @pl.when((i < n_used) & (i % 2 == 0))
    def _():
        step(xbuf0, xbuf1, ybuf0, ybuf1)

    @pl.when((i < n_used) & (i % 2 == 1))
    def _():
        step(xbuf1, xbuf0, ybuf1, ybuf0)


def _experts(x, row_map, blk_e, n_used, w1, b1, w2, b2, layer):
    N, D = x.shape
    assert N & (N - 1) == 0, "token count must be a power of two (row map decode)"
    nb = row_map.shape[0] // MOE_BLOCK
    _, E, _, F2 = w1.shape
    F = w2.shape[2]
    spare = N * TOP_K + jnp.arange(MOE_BLOCK, dtype=jnp.int32)
    dst3 = jnp.concatenate([spare, row_map]).reshape(nb + 1, 1, MOE_BLOCK)
    tok3 = (row_map & (N - 1)).reshape(nb, 1, MOE_BLOCK)
    wspec = lambda r, c: pl.BlockSpec((None, None, r, c), lambda i, be, nu: (layer, be[i], 0, 0))
    mspec = lambda off, hi: pl.BlockSpec(
        (None, 1, MOE_BLOCK), lambda i, be, nu: (jnp.minimum(i + off, hi), 0, 0),
        memory_space=pltpu.SMEM)
    grid_spec = pltpu.PrefetchScalarGridSpec(
        num_scalar_prefetch=2,
        grid=(nb,),
        in_specs=[
            mspec(0, nb), mspec(1, nb), mspec(0, nb - 1), mspec(1, nb - 1),
            pl.BlockSpec(memory_space=pl.ANY),
            wspec(D, F2), wspec(1, F2), wspec(F, D), wspec(1, D),
        ],
        out_specs=pl.BlockSpec(memory_space=pl.ANY),
        scratch_shapes=[pltpu.VMEM((MOE_BLOCK, D), F32), pltpu.VMEM((MOE_BLOCK, D), F32),
                        pltpu.VMEM((MOE_BLOCK, D), F32), pltpu.VMEM((MOE_BLOCK, D), F32),
                        pltpu.VMEM((D, F2), BF16), pltpu.VMEM((F, D), BF16),
                        pltpu.SemaphoreType.DMA, pltpu.SemaphoreType.DMA],
    )
    nl = w1.shape[0]
    return pl.pallas_call(
        _expert_kernel,
        grid_spec=grid_spec,
        out_shape=jax.ShapeDtypeStruct((N * TOP_K + MOE_BLOCK, D), F32),
        compiler_params=_cparams(("arbitrary",)),
        name="moe_experts",
    )(blk_e, n_used, dst3, dst3, tok3, tok3, x, w1, b1.reshape(nl, E, 1, F2), w2, b2.reshape(nl, E, 1, D))


def _combine_ln_kernel(gate_ref, x_ref, y0_ref, y1_ref, y2_ref, y3_ref, g_ref, b_ref, *rest, kh):
    gate = gate_ref[...]
    f = jnp.zeros(x_ref.shape, F32)
    for j, y_ref in enumerate((y0_ref, y1_ref, y2_ref, y3_ref)):
        f = f + y_ref[...] * gate[:, j:j + 1]
    y = _layer_norm(DN_ALPHA * x_ref[...] + f, g_ref[...], b_ref[...])
    if len(rest) == 1:
        rest[0][...] = y
    else:
        w_refs, o_ref, proj_refs = rest[:7], rest[7], rest[8:]
        o_ref[...] = y
        _gla_project(y, *w_refs, *proj_refs, kh)


def _combine_ln(xt, y4, gates, g, b, next_gla=None):
    N, D = xt.shape
    T = MOE_TOK_TILE
    nt = N // T
    yspec = lambda j: pl.BlockSpec((T, D), lambda i: (j * nt + i, 0))
    in_specs = [
        pl.BlockSpec((T, TOP_K), lambda i: (i, 0)),
        pl.BlockSpec((T, D), lambda i: (i, 0)),
        yspec(0), yspec(1), yspec(2), yspec(3),
        pl.BlockSpec((1, D), lambda i: (0, 0)),
        pl.BlockSpec((1, D), lambda i: (0, 0)),
    ]
    args = [gates, xt, y4, y4, y4, y4, g.reshape(1, D), b.reshape(1, D)]
    out_specs = [pl.BlockSpec((T, D), lambda i: (i, 0))]
    out_shape = [jax.ShapeDtypeStruct((N, D), F32)]
    kh = None
    if next_gla is not None:
        ws, w_specs, p_specs, p_shape, kh = _gla_proj_operands(N, D, T, *next_gla)
        in_specs, args = in_specs + w_specs, args + ws
        out_specs, out_shape = out_specs + p_specs, out_shape + p_shape
    outs = pl.pallas_call(
        functools.partial(_combine_ln_kernel, kh=kh),
        grid=(nt,),
        in_specs=in_specs,
        out_specs=out_specs,
        out_shape=out_shape,
        compiler_params=_cparams(("parallel",)),
        name="moe_combine_ln",
    )(*args)
    return outs[0], (tuple(outs[1:]) if next_gla is not None else None)


def _moe_ln(xt, route, w1, b1, w2, b2, g, b, layer, next_gla=None):
    N, D = xt.shape
    idx, gates, rank, counts = route
    counts = counts.reshape(N_EXPERTS).astype(jnp.int32)
    padded = ((counts + MOE_BLOCK - 1) // MOE_BLOCK) * MOE_BLOCK
    pend = jnp.cumsum(padded)
    pstart = pend - padded
    n_blocks = -(-(N * TOP_K) // MOE_BLOCK) + N_EXPERTS
    hot = idx[:, :, None] == jnp.arange(N_EXPERTS, dtype=jnp.int32)
    dest = jnp.sum(jnp.where(hot, pstart, 0), axis=-1) + rank
    blk_start = jnp.arange(n_blocks, dtype=jnp.int32) * MOE_BLOCK
    blk_e = jnp.minimum(jnp.sum((pend[None, :] <= blk_start[:, None]).astype(jnp.int32), axis=1),
                        N_EXPERTS - 1)
    n_used = (pend[-1:] // MOE_BLOCK).astype(jnp.int32)
    row_map = _rowmap(dest.T.reshape(N * TOP_K), N, n_blocks)
    y4 = _experts(xt, row_map, blk_e, n_used, w1, b1, w2, b2, layer)
    return _combine_ln(xt, y4, gates.T, g, b, next_gla)


def _dsa_layer(x, w_in, kv_norm, w_uk, w_uv, w_out, g, b, w_router, b_router):
    Bsz, L, D = x.shape
    qlat, c, ct, qi, ki, wi = _dsa_proj(x, w_in, kv_norm, w_uk)
    o = _dsa_attention(qi, wi, ki, qlat, ct, c, w_uv)
    return _outproj_ln(o.reshape(Bsz * L, -1), w_out, x.reshape(Bsz * L, D), g, b, w_router, b_router)


def _gla_layer(x, proj, norm_g, w_out, g, b, w_router, b_router):
    Bsz, L, D = x.shape
    xt = x.reshape(Bsz * L, D)
    q, k, v, la, r = proj
    sh = lambda t: t.reshape(Bsz, L, t.shape[-1])
    o = _gla_core(sh(q), sh(k), sh(v), sh(la), sh(r), norm_g)
    return _outproj_ln(o.reshape(Bsz * L, -1), w_out, xt, g, b, w_router, b_router)


def kernel(x, a_w_in, a_kv_norm, a_w_uk, a_w_uv, a_w_out, b_w_in, b_w_g2, b_g_bias, b_norm, b_w_out, m_w_router, m_b_router, m_w1, m_b1, m_w2, m_b2, ln1_g, ln1_b, ln2_g, ln2_b):
    Bsz, L, D = x.shape
    proj = None
    for i in range(DEPTH):
        j = i // 2
        if i % 2 == 0:
            xt, route = _dsa_layer(x, a_w_in[j], a_kv_norm[j], a_w_uk[j], a_w_uv[j], a_w_out[j],
                                   ln1_g[i], ln1_b[i], m_w_router[i], m_b_router[i])
        else:
            xt, route = _gla_layer(x, proj, b_norm[j], b_w_out[j], ln1_g[i], ln1_b[i],
                                   m_w_router[i], m_b_router[i])
        nj = (i + 1) // 2
        next_gla = (b_w_in[nj], b_w_g2[nj], b_g_bias[nj]) if (i + 1 < DEPTH and i % 2 == 0) else None
        xt, proj = _moe_ln(xt, route, m_w1, m_b1, m_w2, m_b2, ln2_g[i], ln2_b[i], i, next_gla)
        x = xt.reshape(Bsz, L, D)
    return x
```

```python
import functools

import jax
import jax.numpy as jnp
from jax import lax
from jax.experimental import pallas as pl
from jax.experimental.pallas import tpu as pltpu

F32 = jnp.float32
BF16 = jnp.bfloat16

DEPTH = 2
DN_ALPHA = (2.0 * DEPTH) ** 0.25
LN_EPS = 1e-5
RMS_EPS = 1e-6

A_HEADS = 16
A_QK_DIM = 64
A_V_DIM = 64
A_LATENT = 256
IDX_HEADS = 8
IDX_DIM = 64
IDX_TOPK_MAX = 256
A_QBLK = 128
A_KBLK = 512
A_HGRP = 2
A_ONES = 16
CNT_ROWS = 32
BIS_STEPS = 4
BIS_MAX_STEPS = 36
BIS_FIRST_PIVOT = 0.0625
LOG2E = 1.4426950408889634

B_HEADS = 4
B_GATE_RANK = 16
B_GATE_TAU = 16.0
B_CHUNK = 64
B_SUB = 16
B_TBLK = 512

N_EXPERTS = 32
TOP_K = 4
SWIGLU_LIMIT = 7.0
SWIGLU_ALPHA = 1.702
MOE_BLOCK = 256
ROW_TILE = 512
MOE_TOK_TILE = 256

VMEM_LIMIT = 56 * 1024 * 1024
NEG_BIG = -1e30


def _cparams(sem):
    return pltpu.CompilerParams(dimension_semantics=sem, vmem_limit_bytes=VMEM_LIMIT)


def _dot(a, b):
    return jnp.dot(a, b, preferred_element_type=F32)


def _dot_nt(a, b):
    return lax.dot_general(a, b, (((1,), (1,)), ((), ())), preferred_element_type=F32)


def _dot_tn(a, b):
    return lax.dot_general(a, b, (((0,), (0,)), ((), ())), preferred_element_type=F32)


def _layer_norm(z, g, b):
    mu = jnp.mean(z, axis=-1, keepdims=True)
    zc = z - mu
    var = jnp.mean(zc * zc, axis=-1, keepdims=True)
    return zc * lax.rsqrt(var + LN_EPS) * g + b


def _dsa_proj_kernel(x_ref, wqt_ref, wc_ref, wqit_ref, wki_ref, wwit_ref, kvn_ref, wukt_ref,
                     qlat_ref, c_ref, ct_ref, qi_ref, ki_ref, wi_ref):
    T = x_ref.shape[0]
    QB = A_QBLK
    xb = x_ref[...].astype(BF16)
    qt = _dot_nt(wqt_ref[...], xb)
    for h in range(A_HEADS):
        qh = qt[h * A_QK_DIM:(h + 1) * A_QK_DIM, :].astype(BF16)
        ql = (_dot(wukt_ref[h], qh) * (A_QK_DIM ** -0.5 * LOG2E)).astype(BF16)
        for j in range(T // QB):
            qlat_ref[j, :, h * QB:(h + 1) * QB] = ql[:, j * QB:(j + 1) * QB]
    c = _dot(xb, wc_ref[...])
    c = c * lax.rsqrt(jnp.mean(c * c, axis=-1, keepdims=True) + RMS_EPS) * kvn_ref[...]
    c_ref[...] = c.astype(BF16)
    ct_ref[:A_LATENT, :] = c.T.astype(BF16)
    ct_ref[A_LATENT:, :] = jnp.ones((A_ONES, T), BF16)
    qit = _dot_nt(wqit_ref[...], xb) * (IDX_DIM ** -0.5)
    for h in range(IDX_HEADS):
        qh = qit[h * IDX_DIM:(h + 1) * IDX_DIM, :].astype(BF16)
        for j in range(T // QB):
            qi_ref[j, :, h * QB:(h + 1) * QB] = qh[:, j * QB:(j + 1) * QB]
    ki_ref[...] = _dot(xb, wki_ref[...]).astype(BF16)
    wi_ref[...] = _dot_nt(wwit_ref[...], xb) * (IDX_HEADS ** -0.5)


def _dsa_proj(x, w_in, kv_norm, w_uk):
    Bsz, L, D = x.shape
    T, QB = A_KBLK, A_QBLK
    nkb, nqb, qpt = L // T, L // QB, T // QB
    o1 = A_HEADS * A_QK_DIM
    o2 = o1 + A_LATENT
    o3 = o2 + IDX_HEADS * IDX_DIM
    o4 = o3 + IDX_DIM
    wb = w_in.astype(BF16)
    wqt, wc, wqit, wki, wwit = wb[:, :o1].T, wb[:, o1:o2], wb[:, o2:o3].T, wb[:, o3:o4], wb[:, o4:].T
    wukt = jnp.swapaxes(w_uk, 1, 2).astype(BF16)
    full = lambda shape: pl.BlockSpec(shape, lambda b, t: (0,) * len(shape))
    return pl.pallas_call(
        _dsa_proj_kernel,
        grid=(Bsz, nkb),
        in_specs=[
            pl.BlockSpec((None, T, D), lambda b, t: (b, t, 0)),
            full(wqt.shape), full(wc.shape), full(wqit.shape), full(wki.shape), full(wwit.shape),
            full((1, A_LATENT)), full(wukt.shape),
        ],
        out_specs=[
            pl.BlockSpec((None, qpt, A_LATENT, A_HEADS * QB), lambda b, t: (b, t, 0, 0)),
            pl.BlockSpec((None, None, T, A_LATENT), lambda b, t: (b, t, 0, 0)),
            pl.BlockSpec((None, None, A_LATENT + A_ONES, T), lambda b, t: (b, t, 0, 0)),
            pl.BlockSpec((None, qpt, IDX_DIM, IDX_HEADS * QB), lambda b, t: (b, t, 0, 0)),
            pl.BlockSpec((None, None, T, IDX_DIM), lambda b, t: (b, t, 0, 0)),
            pl.BlockSpec((None, IDX_HEADS, T), lambda b, t: (b, 0, t)),
        ],
        out_shape=[
            jax.ShapeDtypeStruct((Bsz, nqb, A_LATENT, A_HEADS * QB), BF16),
            jax.ShapeDtypeStruct((Bsz, nkb, T, A_LATENT), BF16),
            jax.ShapeDtypeStruct((Bsz, nkb, A_LATENT + A_ONES, T), BF16),
            jax.ShapeDtypeStruct((Bsz, nqb, IDX_DIM, IDX_HEADS * QB), BF16),
            jax.ShapeDtypeStruct((Bsz, nkb, T, IDX_DIM), BF16),
            jax.ShapeDtypeStruct((Bsz, IDX_HEADS, L), F32),
        ],
        compiler_params=_cparams(("parallel", "parallel")),
        name="dsa_proj",
    )(x, wqt, wc, wqit, wki, wwit, kv_norm.reshape(1, A_LATENT), wukt)


def _ordered_int_to_float(o):
    bits = jnp.where(o >= 0, o, o ^ jnp.int32(2 ** 31 - 1))
    return lax.bitcast_convert_type(bits, F32)


def _float_to_ordered_int(x):
    bits = lax.bitcast_convert_type(jnp.asarray(x, F32), jnp.int32)
    return jnp.where(bits >= 0, bits, bits ^ jnp.int32(2 ** 31 - 1))


def _dsa_attn_kernel(qi_ref, wi_ref, ki_ref, qlat_ref, ct_ref, c_ref, wuvt_ref, o_ref,
                     sc_ref, s_ref, m_ref, acc_ref, ot_ref, jcut_ref, *, topk, seq_len):
    QB, KB = A_QBLK, A_KBLK
    qb = pl.program_id(1)
    q0 = qb * QB
    nvb = (q0 + QB + KB - 1) // KB
    qpos = q0 + lax.broadcasted_iota(jnp.int32, (1, QB), 1)
    krow = lax.broadcasted_iota(jnp.int32, (KB, QB), 0)
    kf = float(topk)

    wi = wi_ref[...]
    qi = qi_ref[...]

    def score_body(kb, carry):
        s = _dot(ki_ref[kb], qi)
        sc = jnp.zeros((KB, QB), F32)
        for h in range(IDX_HEADS):
            sc = sc + jnp.maximum(s[:, h * QB:(h + 1) * QB], 0.0) * wi[h:h + 1, :]
        sc_ref[kb] = jnp.where((kb * KB + krow) <= qpos, sc, -jnp.inf)
        return carry

    lax.fori_loop(0, nvb, score_body, 0)

    def count(pred):
        def body(kb, acc):
            hit = jnp.where(pred(sc_ref[kb], kb), 1.0, 0.0)
            return acc + jnp.sum(hit.reshape(KB // CNT_ROWS, CNT_ROWS, QB), axis=0)
        acc = lax.fori_loop(0, nvb, body, jnp.zeros((CNT_ROWS, QB), F32))
        return jnp.sum(acc, axis=0, keepdims=True)

    def row_max():
        def body(kb, acc):
            return jnp.maximum(acc, jnp.max(sc_ref[kb].reshape(KB // CNT_ROWS, CNT_ROWS, QB), axis=0))
        acc = lax.fori_loop(0, nvb, body, jnp.full((CNT_ROWS, QB), -jnp.inf, F32))
        return jnp.max(acc, axis=0, keepdims=True)

    def half_gap(lo, hi):
        return lax.shift_right_logical(hi - lo, jnp.int32(1))

    def bis_step(st, mid):
        lo, hi, thr, done = st
        cand = _ordered_int_to_float(mid)
        cnt = count(lambda blk, kb: blk >= cand)
        ge = cnt >= kf
        hit = (cnt == kf) & (done < 0.5)
        return (jnp.where(ge, mid, lo), jnp.where(ge, hi, mid),
                jnp.where(hit, cand, thr), jnp.where(hit, 1.0, done))

    def bis_cond(st):
        i, lo, hi, _, done = st
        open_ = jnp.where((done < 0.5) & (half_gap(lo, hi) > 0), 1.0, 0.0)
        return (i < BIS_MAX_STEPS) & (jnp.max(open_) > 0.5)

    def bis_body(st):
        i, st = st[0], st[1:]
        for _ in range(BIS_STEPS):
            st = bis_step(st, st[0] + half_gap(st[0], st[1]))
        return (i + BIS_STEPS,) + st

    smax = row_max()
    lo0 = jnp.full((1, QB), _float_to_ordered_int(-jnp.inf), jnp.int32)
    hi0 = _float_to_ordered_int(smax) + 1
    done0 = jnp.where(qpos < topk, 1.0, 0.0)
    st = (lo0, hi0, jnp.full((1, QB), -jnp.inf, F32), done0)
    first = jnp.where(smax > 0.0, _float_to_ordered_int(smax * BIS_FIRST_PIVOT),
                      lo0 + half_gap(lo0, hi0))
    st = bis_step(st, first)
    _, u, _, thr, done = lax.while_loop(bis_cond, bis_body, (jnp.int32(0),) + st)
    thr = jnp.where(done > 0.5, thr, _ordered_int_to_float(u))
    jcut_ref[...] = jnp.full((1, QB), seq_len, jnp.int32)

    @pl.when(jnp.min(done) < 0.5)
    def _():
        need = kf - count(lambda blk, kb: blk > thr)
        nbits = max(1, (seq_len - 1).bit_length())

        def jbody(i, x):
            trial = x | jnp.left_shift(jnp.int32(1), nbits - 1 - i)
            g = count(lambda blk, kb: (blk == thr) & ((kb * KB + krow) < trial))
            return jnp.where(g < need, trial, x)

        x = lax.fori_loop(0, nbits, jbody, jnp.zeros((1, QB), jnp.int32))
        jcut_ref[...] = jnp.where(done > 0.5, seq_len, x)

    jcut = jcut_ref[...]

    m_ref[...] = jnp.full(m_ref.shape, NEG_BIG, F32)
    acc_ref[...] = jnp.zeros(acc_ref.shape, F32)

    pair_cols = [slice(hp * A_HGRP * QB, (hp + 1) * A_HGRP * QB) for hp in range(A_HEADS // A_HGRP)]

    def logits(kb, cols):
        return _dot(c_ref[kb], qlat_ref[:, cols])

    def mask_bias(kb):
        blk = sc_ref[kb]
        kpos = kb * KB + krow
        sel = ((blk > thr) | ((blk == thr) & (kpos <= jcut))) & (kpos <= qpos)
        bias = jnp.where(sel, 0.0, NEG_BIG)
        return jnp.concatenate([bias] * A_HGRP, axis=1)

    def consume(kb, s, cols, bias2):
        s = s + bias2
        m_prev = m_ref[:, cols]
        m_new = jnp.maximum(m_prev, jnp.max(s, axis=0, keepdims=True))
        alpha = jnp.exp2(m_prev - m_new)
        p = jnp.exp2(s - m_new).astype(BF16)
        acc_ref[:, cols] = alpha * acc_ref[:, cols] + _dot(ct_ref[kb], p)
        m_ref[:, cols] = m_new

    for cols in pair_cols:
        s_ref[:, cols] = logits(0, cols)

    def attn_body(kb, carry):
        bias2 = mask_bias(kb)
        for cols in pair_cols:
            s = s_ref[:, cols]
            s_ref[:, cols] = logits(kb + 1, cols)
            consume(kb, s, cols, bias2)
        return carry

    lax.fori_loop(0, nvb - 1, attn_body, 0)
    bias2 = mask_bias(nvb - 1)
    for cols in pair_cols:
        consume(nvb - 1, s_ref[:, cols], cols, bias2)

    for h in range(A_HEADS):
        cols = slice(h * QB, (h + 1) * QB)
        oh = acc_ref[:A_LATENT, cols] / acc_ref[A_LATENT:A_LATENT + 1, cols]
        ot_ref[h * A_V_DIM:(h + 1) * A_V_DIM, :] = _dot(wuvt_ref[h], oh.astype(BF16))
    o_ref[...] = ot_ref[...].T.astype(BF16)


def _dsa_attention(qi, wi, ki, qlat, ct, c, w_uv):
    Bsz, nqb, _, _ = qlat.shape
    QB, KB = A_QBLK, A_KBLK
    L = nqb * QB
    nkb = L // KB
    topk = min(IDX_TOPK_MAX, L // 4)
    wuvt = jnp.swapaxes(w_uv, 1, 2).astype(BF16)
    kern = functools.partial(_dsa_attn_kernel, topk=topk, seq_len=L)
    return pl.pallas_call(
        kern,
        grid=(Bsz, nqb),
        in_specs=[
            pl.BlockSpec((None, None, IDX_DIM, IDX_HEADS * QB), lambda b, q: (b, q, 0, 0)),
            pl.BlockSpec((None, IDX_HEADS, QB), lambda b, q: (b, 0, q)),
            pl.BlockSpec((None, nkb, KB, IDX_DIM), lambda b, q: (b, 0, 0, 0)),
            pl.BlockSpec((None, None, A_LATENT, A_HEADS * QB), lambda b, q: (b, q, 0, 0)),
            pl.BlockSpec((None, nkb, A_LATENT + A_ONES, KB), lambda b, q: (b, 0, 0, 0)),
            pl.BlockSpec((None, nkb, KB, A_LATENT), lambda b, q: (b, 0, 0, 0)),
            pl.BlockSpec((A_HEADS, A_V_DIM, A_LATENT), lambda b, q: (0, 0, 0)),
        ],
        out_specs=pl.BlockSpec((None, QB, A_HEADS * A_V_DIM), lambda b, q: (b, q, 0)),
        out_shape=jax.ShapeDtypeStruct((Bsz, L, A_HEADS * A_V_DIM), BF16),
        scratch_shapes=[
            pltpu.VMEM((nkb, KB, QB), F32),
            pltpu.VMEM((KB, A_HEADS * QB), F32),
            pltpu.VMEM((1, A_HEADS * QB), F32),
            pltpu.VMEM((A_LATENT + A_ONES, A_HEADS * QB), F32),
            pltpu.VMEM((A_HEADS * A_V_DIM, QB), F32),
            pltpu.VMEM((1, QB), jnp.int32),
        ],
        compiler_params=_cparams(("parallel", "parallel")),
        name="dsa_attn",
    )(qi, wi, ki, qlat, ct, c, wuvt)


def _outproj_ln_kernel(o_ref, w_ref, x_ref, g_ref, b_ref, wrt_ref, br_ref,
                       y_ref, idx_ref, gate_ref, rank_ref, cnt_ref, carry_ref):
    h = _dot(o_ref[...], w_ref[...])
    y = _layer_norm(DN_ALPHA * x_ref[...] + h, g_ref[...], b_ref[...])
    y_ref[...] = y
    _route(y, wrt_ref, br_ref, idx_ref, gate_ref, rank_ref, cnt_ref, carry_ref)


def _outproj_ln(o, w_out, x, g, b, w_router, b_router):
    N, K = o.shape
    D = x.shape[1]
    T = ROW_TILE
    tok = pl.BlockSpec((TOP_K, T), lambda i: (0, i))
    outs = pl.pallas_call(
        _outproj_ln_kernel,
        grid=(N // T,),
        in_specs=[
            pl.BlockSpec((T, K), lambda i: (i, 0)),
            pl.BlockSpec((K, D), lambda i: (0, 0)),
            pl.BlockSpec((T, D), lambda i: (i, 0)),
            pl.BlockSpec((1, D), lambda i: (0, 0)),
            pl.BlockSpec((1, D), lambda i: (0, 0)),
            pl.BlockSpec((N_EXPERTS, D), lambda i: (0, 0)),
            pl.BlockSpec((N_EXPERTS, 1), lambda i: (0, 0)),
        ],
        out_specs=[pl.BlockSpec((T, D), lambda i: (i, 0)), tok, tok, tok,
                   pl.BlockSpec((N_EXPERTS, 1), lambda i: (0, 0))],
        out_shape=[
            jax.ShapeDtypeStruct((N, D), F32),
            jax.ShapeDtypeStruct((TOP_K, N), jnp.int32),
            jax.ShapeDtypeStruct((TOP_K, N), F32),
            jax.ShapeDtypeStruct((TOP_K, N), jnp.int32),
            jax.ShapeDtypeStruct((N_EXPERTS, 1), F32),
        ],
        scratch_shapes=[pltpu.VMEM((N_EXPERTS, 1), F32)],
        compiler_params=_cparams(("arbitrary",)),
        name="outproj_ln_route",
    )(o, w_out.astype(BF16), x, g.reshape(1, D), b.reshape(1, D),
      w_router.T, b_router.reshape(N_EXPERTS, 1))
    return outs[0], tuple(outs[1:])


def _gla_project(x, wq_ref, wk_ref, wv_ref, wg_ref, wr_ref, wg2_ref, gb_ref,
                 q_ref, k_ref, v_ref, la_ref, r_ref, kh):
    xb = x.astype(BF16)
    q_ref[...] = _dot(xb, wq_ref[...]) * (kh ** -0.5)
    k_ref[...] = _dot(xb, wk_ref[...])
    v_ref[...] = _dot(xb, wv_ref[...]).astype(BF16)
    r_ref[...] = _dot(xb, wr_ref[...])
    glr = _dot(xb, wg_ref[...])
    z = _dot(glr.astype(BF16), wg2_ref[...]) + gb_ref[...]
    log_sig = jnp.minimum(z, 0.0) - jnp.log(1.0 + jnp.exp(-jnp.abs(z)))
    la_ref[...] = log_sig / B_GATE_TAU


def _gla_proj_operands(N, D, T, w_in, w_g2, g_bias):
    DK = w_g2.shape[1]
    DV = D
    wb = w_in.astype(BF16)
    o1, o2, o3, o4 = DK, 2 * DK, 2 * DK + DV, 2 * DK + DV + B_GATE_RANK
    ws = [wb[:, :o1], wb[:, o1:o2], wb[:, o2:o3], wb[:, o3:o4], wb[:, o4:],
          w_g2.astype(BF16), g_bias.reshape(1, DK)]
    full = lambda shape: pl.BlockSpec(shape, lambda i: (0,) * len(shape))
    row = lambda w: pl.BlockSpec((T, w), lambda i: (i, 0))
    out_specs = [row(DK), row(DK), row(DV), row(DK), row(DV)]
    out_shape = [
        jax.ShapeDtypeStruct((N, DK), F32),
        jax.ShapeDtypeStruct((N, DK), F32),
        jax.ShapeDtypeStruct((N, DV), BF16),
        jax.ShapeDtypeStruct((N, DK), F32),
        jax.ShapeDtypeStruct((N, DV), F32),
    ]
    return ws, [full(w.shape) for w in ws], out_specs, out_shape, DK // B_HEADS


def _split3(a):
    hi = a.astype(BF16)
    r1 = a - hi.astype(F32)
    mid = r1.astype(BF16)
    lo = (r1 - mid.astype(F32)).astype(BF16)
    return hi, mid, lo


def _gla_kernel(q_ref, k_ref, v_ref, la_ref, r_ref, ng_ref, o_ref, st_ref, a_ref):
    C, SB = B_CHUNK, B_SUB
    nsub = C // SB
    kh = q_ref.shape[1] // B_HEADS
    vh = v_ref.shape[1] // B_HEADS

    @pl.when(pl.program_id(1) == 0)
    def _():
        st_ref[...] = jnp.zeros(st_ref.shape, F32)

    ri = lax.broadcasted_iota(jnp.int32, (C, C), 0)
    ci = lax.broadcasted_iota(jnp.int32, (C, C), 1)
    tril = jnp.where(ci <= ri, 1.0, 0.0).astype(BF16)
    sub_r = lax.broadcasted_iota(jnp.int32, (SB, 1), 0)
    sub_c = lax.broadcasted_iota(jnp.int32, (1, SB), 1)

    def head_chunk(rows, h):
        kc = slice(h * kh, (h + 1) * kh)
        vc = slice(h * vh, (h + 1) * vh)
        q = q_ref[rows, kc]
        k = k_ref[rows, kc]
        v = v_ref[rows, vc]
        la = la_ref[rows, kc]
        hi, mid, lo = _split3(la)
        b = _dot(tril, hi) + _dot(tril, mid) + _dot(tril, lo)
        st = st_ref[h]
        o = _dot_nt((q * jnp.exp(b)).astype(BF16), st.astype(BF16))

        a_ref[h] = jnp.zeros((C, C), F32)
        for I in range(nsub):
            r0 = I * SB
            bI = b[r0:r0 + SB, :]
            qI = q[r0:r0 + SB, :]
            kI = k[r0:r0 + SB, :]
            if I > 0:
                ref_lvl = b[r0 - 1:r0, :]
                qs = (qI * jnp.exp(bI - ref_lvl)).astype(BF16)
                ks = (k[:r0, :] * jnp.exp(ref_lvl - b[:r0, :])).astype(BF16)
                a_ref[h, r0:r0 + SB, :r0] = _dot_nt(qs, ks)
            diag = jnp.zeros((SB, SB), F32)
            for j in range(SB):
                dlt = jnp.where(sub_r >= j, bI - bI[j:j + 1, :], -jnp.inf)
                col = jnp.sum(qI * kI[j:j + 1, :] * jnp.exp(dlt), axis=1, keepdims=True)
                diag = diag + col * jnp.where(sub_c == j, 1.0, 0.0)
            a_ref[h, r0:r0 + SB, r0:r0 + SB] = diag
        o = o + _dot(a_ref[h].astype(BF16), v)

        b_last = b[C - 1:C, :]
        kd = (k * jnp.exp(b_last - b)).astype(BF16)
        st_ref[h] = st * jnp.exp(b_last) + _dot_tn(v, kd)

        o = o * lax.rsqrt(jnp.mean(o * o, axis=-1, keepdims=True) + RMS_EPS) * ng_ref[:, vc]
        r = r_ref[rows, vc]
        o_ref[rows, vc] = (o * (r / (1.0 + jnp.exp(-r)))).astype(BF16)

    def chunk_body(ch, carry):
        rows = pl.ds(pl.multiple_of(ch * C, C), C)
        for h in range(B_HEADS):
            head_chunk(rows, h)
        return carry

    lax.fori_loop(0, q_ref.shape[0] // C, chunk_body, 0, unroll=2)


def _gla_core(q, k, v, la, r, norm_g):
    Bsz, L, DK = q.shape
    DV = v.shape[2]
    kh, vh = DK // B_HEADS, DV // B_HEADS
    T = B_TBLK
    kspec = pl.BlockSpec((None, T, DK), lambda b, t: (b, t, 0))
    vspec = pl.BlockSpec((None, T, DV), lambda b, t: (b, t, 0))
    return pl.pallas_call(
        _gla_kernel,
        grid=(Bsz, L // T),
        in_specs=[kspec, kspec, vspec, kspec, vspec, pl.BlockSpec((1, DV), lambda b, t: (0, 0))],
        out_specs=vspec,
        out_shape=jax.ShapeDtypeStruct((Bsz, L, DV), BF16),
        scratch_shapes=[pltpu.VMEM((B_HEADS, vh, kh), F32),
                        pltpu.VMEM((B_HEADS, B_CHUNK, B_CHUNK), F32)],
        compiler_params=_cparams(("parallel", "arbitrary")),
        name="gla_core",
    )(q, k, v, la, r, norm_g.reshape(1, DV))


def _route(x, wt_ref, b_ref, idx_ref, gate_ref, rank_ref, cnt_ref, carry_ref):
    T = x.shape[0]
    E = N_EXPERTS

    @pl.when(pl.program_id(0) == 0)
    def _():
        carry_ref[...] = jnp.zeros(carry_ref.shape, F32)

    wt = wt_ref[...]
    xh = x.astype(BF16)
    xl = (x - xh.astype(F32)).astype(BF16)
    wh = wt.astype(BF16)
    wl = (wt - wh.astype(F32)).astype(BF16)
    logits = _dot_nt(wh, xh) + (_dot_nt(wl, xh) + _dot_nt(wh, xl)) + b_ref[...]

    eidx = lax.broadcasted_iota(jnp.int32, (E, T), 0)
    work = logits
    vals, idxs = [], []
    for _ in range(TOP_K):
        mx = jnp.max(work, axis=0, keepdims=True)
        ix = jnp.min(jnp.where(work == mx, eidx, E), axis=0, keepdims=True)
        vals.append(mx)
        idxs.append(ix)
        work = jnp.where(eidx == ix, -jnp.inf, work)
    exps = [jnp.exp(vv - vals[0]) for vv in vals]
    den = exps[0] + exps[1] + exps[2] + exps[3]

    onehot = jnp.zeros((E, T), F32)
    for ix in idxs:
        onehot = onehot + jnp.where(eidx == ix, 1.0, 0.0)
    ri = lax.broadcasted_iota(jnp.int32, (T, T), 0)
    ci = lax.broadcasted_iota(jnp.int32, (T, T), 1)
    before = jnp.where(ri < ci, 1.0, 0.0).astype(BF16)
    prefix = _dot(onehot.astype(BF16), before) + carry_ref[...]
    carry_ref[...] = carry_ref[...] + jnp.sum(onehot, axis=1, keepdims=True)
    cnt_ref[...] = carry_ref[...]

    for j in range(TOP_K):
        idx_ref[j:j + 1, :] = idxs[j]
        gate_ref[j:j + 1, :] = exps[j] / den
        rk = jnp.sum(jnp.where(eidx == idxs[j], prefix, 0.0), axis=0, keepdims=True)
        rank_ref[j:j + 1, :] = rk.astype(jnp.int32)


def _rowmap_kernel(dest_ref, init_ref, map_ref, sem, *, n_tok):
    i = pl.program_id(0)
    toks = dest_ref.shape[0] // TOP_K

    @pl.when(i == 0)
    def _():
        cp = pltpu.make_async_copy(init_ref, map_ref, sem)
        cp.start()
        cp.wait()

    t0 = i * toks

    def body(a, c):
        for j in range(TOP_K):
            map_ref[dest_ref[a * TOP_K + j]] = j * n_tok + t0 + a
        return c

    lax.fori_loop(0, toks, body, 0, unroll=8)


def _rowmap(dest_flat, n_tok, n_blocks):
    NK = dest_flat.shape[0]
    tile = min(NK, 8192)
    P = n_blocks * MOE_BLOCK
    init = n_tok * TOP_K + (jnp.arange(P, dtype=jnp.int32) & (MOE_BLOCK - 1))
    return pl.pallas_call(
        functools.partial(_rowmap_kernel, n_tok=n_tok),
        grid=(NK // tile,),
        in_specs=[pl.BlockSpec((tile,), lambda i: (i,), memory_space=pltpu.SMEM),
                  pl.BlockSpec(memory_space=pl.ANY)],
        out_specs=pl.BlockSpec((P,), lambda i: (0,), memory_space=pltpu.SMEM),
        out_shape=jax.ShapeDtypeStruct((P,), jnp.int32),
        scratch_shapes=[pltpu.SemaphoreType.DMA],
        compiler_params=_cparams(("arbitrary",)),
        name="moe_rowmap",
    )(dest_flat, init)


def _expert_kernel(blk_e_ref, nused_ref, dst_prev_ref, dst_cur_ref, tok_cur_ref, tok_next_ref,
                   x_ref, w1_ref, b1_ref, w2_ref, b2_ref, y4_ref,
                   xbuf0, xbuf1, ybuf0, ybuf1, w1b_ref, w2b_ref, gsem, ssem):
    i = pl.program_id(0)
    n_used = nused_ref[0]
    BLK = MOE_BLOCK

    def gather_row(tok_ref, r, xdst):
        return pltpu.make_async_copy(x_ref.at[pl.ds(tok_ref[0, r], 1)], xdst.at[pl.ds(r, 1)], gsem)

    def scatter_row(dst, r, ysrc):
        return pltpu.make_async_copy(ysrc.at[pl.ds(r, 1)], y4_ref.at[pl.ds(dst, 1)], ssem)

    def wait_rows(sem):
        pltpu.make_async_copy(xbuf0, xbuf1, sem).wait()

    @pl.when(i == 0)
    def _():
        ybuf1[...] = jnp.zeros(ybuf1.shape, F32)

        def first(r, c):
            gather_row(tok_cur_ref, r, xbuf0).start()
            return c
        lax.fori_loop(0, BLK, first, 0, unroll=8)

    prev = blk_e_ref[jnp.maximum(i - 1, 0)]
    fresh = (i == 0) | (blk_e_ref[i] != prev)

    @pl.when(fresh & (i < n_used))
    def _():
        w1b_ref[...] = w1_ref[...].astype(BF16)
        w2b_ref[...] = w2_ref[...].astype(BF16)

    def step(xcur, xnext, ycur, yprev):
        wait_rows(gsem)

        @pl.when(i >= 1)
        def _():
            wait_rows(ssem)

        @pl.when(n_used > 0)
        def _():
            for r in range(BLK):
                gather_row(tok_next_ref, r, xnext).start()
                scatter_row(dst_prev_ref[0, r], r, yprev).start()

        F = w2_ref.shape[0]
        xb = xcur[...].astype(BF16)
        h = _dot(xb, w1b_ref[...]) + b1_ref[...]
        g = jnp.minimum(h[:, :F], SWIGLU_LIMIT)
        u = jnp.clip(h[:, F:], -SWIGLU_LIMIT, SWIGLU_LIMIT)
        glu = g / (1.0 + jnp.exp(-SWIGLU_ALPHA * g))
        act = ((u + 1.0) * glu).astype(BF16)
        ycur[...] = _dot(act, w2b_ref[...]) + b2_ref[...]

        @pl.when(i == n_used - 1)
        def _():
            wait_rows(gsem)
            wait_rows(ssem)

            def last(r, c):
                scatter_row(dst_cur_ref[0, r], r, ycur).start()
                return c
            lax.fori_loop(0, BLK, last, 0, unroll=8)
            wait_rows(ssem)

    @pl.when((i < n_used) & (i % 2 == 0))
    def _():
        step(xbuf0, xbuf1, ybuf0, ybuf1)

    @pl.when((i < n_used) & (i % 2 == 1))
    def _():
        step(xbuf1, xbuf0, ybuf1, ybuf0)


def _experts(x, row_map, blk_e, n_used, w1, b1, w2, b2, layer):
    N, D = x.shape
    assert N & (N - 1) == 0, "token count must be a power of two (row map decode)"
    nb = row_map.shape[0] // MOE_BLOCK
    _, E, _, F2 = w1.shape
    F = w2.shape[2]
    spare = N * TOP_K + jnp.arange(MOE_BLOCK, dtype=jnp.int32)
    dst3 = jnp.concatenate([spare, row_map]).reshape(nb + 1, 1, MOE_BLOCK)
    tok3 = (row_map & (N - 1)).reshape(nb, 1, MOE_BLOCK)
    wspec = lambda r, c: pl.BlockSpec((None, None, r, c), lambda i, be, nu: (layer, be[i], 0, 0))
    mspec = lambda off, hi: pl.BlockSpec(
        (None, 1, MOE_BLOCK), lambda i, be, nu: (jnp.minimum(i + off, hi), 0, 0),
        memory_space=pltpu.SMEM)
    grid_spec = pltpu.PrefetchScalarGridSpec(
        num_scalar_prefetch=2,
        grid=(nb,),
        in_specs=[
            mspec(0, nb), mspec(1, nb), mspec(0, nb - 1), mspec(1, nb - 1),
            pl.BlockSpec(memory_space=pl.ANY),
            wspec(D, F2), wspec(1, F2), wspec(F, D), wspec(1, D),
        ],
        out_specs=pl.BlockSpec(memory_space=pl.ANY),
        scratch_shapes=[pltpu.VMEM((MOE_BLOCK, D), F32), pltpu.VMEM((MOE_BLOCK, D), F32),
                        pltpu.VMEM((MOE_BLOCK, D), F32), pltpu.VMEM((MOE_BLOCK, D), F32),
                        pltpu.VMEM((D, F2), BF16), pltpu.VMEM((F, D), BF16),
                        pltpu.SemaphoreType.DMA, pltpu.SemaphoreType.DMA],
    )
    nl = w1.shape[0]
    return pl.pallas_call(
        _expert_kernel,
        grid_spec=grid_spec,
        out_shape=jax.ShapeDtypeStruct((N * TOP_K + MOE_BLOCK, D), F32),
        compiler_params=_cparams(("arbitrary",)),
        name="moe_experts",
    )(blk_e, n_used, dst3, dst3, tok3, tok3, x, w1, b1.reshape(nl, E, 1, F2), w2, b2.reshape(nl, E, 1, D))


def _combine_ln_kernel(gate_ref, x_ref, y0_ref, y1_ref, y2_ref, y3_ref, g_ref, b_ref, *rest, kh):
    gate = gate_ref[...]
    f = jnp.zeros(x_ref.shape, F32)
    for j, y_ref in enumerate((y0_ref, y1_ref, y2_ref, y3_ref)):
        f = f + y_ref[...] * gate[:, j:j + 1]
    y = _layer_norm(DN_ALPHA * x_ref[...] + f, g_ref[...], b_ref[...])
    if len(rest) == 1:
        rest[0][...] = y
    else:
        w_refs, o_ref, proj_refs = rest[:7], rest[7], rest[8:]
        o_ref[...] = y
        _gla_project(y, *w_refs, *proj_refs, kh)


def _combine_ln(xt, y4, gates, g, b, next_gla=None):
    N, D = xt.shape
    T = MOE_TOK_TILE
    nt = N // T
    yspec = lambda j: pl.BlockSpec((T, D), lambda i: (j * nt + i, 0))
    in_specs = [
        pl.BlockSpec((T, TOP_K), lambda i: (i, 0)),
        pl.BlockSpec((T, D), lambda i: (i, 0)),
        yspec(0), yspec(1), yspec(2), yspec(3),
        pl.BlockSpec((1, D), lambda i: (0, 0)),
        pl.BlockSpec((1, D), lambda i: (0, 0)),
    ]
    args = [gates, xt, y4, y4, y4, y4, g.reshape(1, D), b.reshape(1, D)]
    out_specs = [pl.BlockSpec((T, D), lambda i: (i, 0))]
    out_shape = [jax.ShapeDtypeStruct((N, D), F32)]
    kh = None
    if next_gla is not None:
        ws, w_specs, p_specs, p_shape, kh = _gla_proj_operands(N, D, T, *next_gla)
        in_specs, args = in_specs + w_specs, args + ws
        out_specs, out_shape = out_specs + p_specs, out_shape + p_shape
    outs = pl.pallas_call(
        functools.partial(_combine_ln_kernel, kh=kh),
        grid=(nt,),
        in_specs=in_specs,
        out_specs=out_specs,
        out_shape=out_shape,
        compiler_params=_cparams(("parallel",)),
        name="moe_combine_ln",
    )(*args)
    return outs[0], (tuple(outs[1:]) if next_gla is not None else None)


def _moe_ln(xt, route, w1, b1, w2, b2, g, b, layer, next_gla=None):
    N, D = xt.shape
    idx, gates, rank, counts = route
    counts = counts.reshape(N_EXPERTS).astype(jnp.int32)
    padded = ((counts + MOE_BLOCK - 1) // MOE_BLOCK) * MOE_BLOCK
    pend = jnp.cumsum(padded)
    pstart = pend - padded
    n_blocks = -(-(N * TOP_K) // MOE_BLOCK) + N_EXPERTS
    hot = idx[:, :, None] == jnp.arange(N_EXPERTS, dtype=jnp.int32)
    dest = jnp.sum(jnp.where(hot, pstart, 0), axis=-1) + rank
    blk_start = jnp.arange(n_blocks, dtype=jnp.int32) * MOE_BLOCK
    blk_e = jnp.minimum(jnp.sum((pend[None, :] <= blk_start[:, None]).astype(jnp.int32), axis=1),
                        N_EXPERTS - 1)
    n_used = (pend[-1:] // MOE_BLOCK).astype(jnp.int32)
    row_map = _rowmap(dest.T.reshape(N * TOP_K), N, n_blocks)
    y4 = _experts(xt, row_map, blk_e, n_used, w1, b1, w2, b2, layer)
    return _combine_ln(xt, y4, gates.T, g, b, next_gla)


def _dsa_layer(x, w_in, kv_norm, w_uk, w_uv, w_out, g, b, w_router, b_router):
    Bsz, L, D = x.shape
    qlat, c, ct, qi, ki, wi = _dsa_proj(x, w_in, kv_norm, w_uk)
    o = _dsa_attention(qi, wi, ki, qlat, ct, c, w_uv)
    return _outproj_ln(o.reshape(Bsz * L, -1), w_out, x.reshape(Bsz * L, D), g, b, w_router, b_router)


def _gla_layer(x, proj, norm_g, w_out, g, b, w_router, b_router):
    Bsz, L, D = x.shape
    xt = x.reshape(Bsz * L, D)
    q, k, v, la, r = proj
    sh = lambda t: t.reshape(Bsz, L, t.shape[-1])
    o = _gla_core(sh(q), sh(k), sh(v), sh(la), sh(r), norm_g)
    return _outproj_ln(o.reshape(Bsz * L, -1), w_out, xt, g, b, w_router, b_router)


def kernel(x, a_w_in, a_kv_norm, a_w_uk, a_w_uv, a_w_out, b_w_in, b_w_g2, b_g_bias, b_norm, b_w_out, m_w_router, m_b_router, m_w1, m_b1, m_w2, m_b2, ln1_g, ln1_b, ln2_g, ln2_b):
    Bsz, L, D = x.shape
    proj = None
    for i in range(DEPTH):
        j = i // 2
        if i % 2 == 0:
            xt, route = _dsa_layer(x, a_w_in[j], a_kv_norm[j], a_w_uk[j], a_w_uv[j], a_w_out[j],
                                   ln1_g[i], ln1_b[i], m_w_router[i], m_b_router[i])
        else:
            xt, route = _gla_layer(x, proj, b_norm[j], b_w_out[j], ln1_g[i], ln1_b[i],
                                   m_w_router[i], m_b_router[i])
        nj = (i + 1) // 2
        next_gla = (b_w_in[nj], b_w_g2[nj], b_g_bias[nj]) if (i + 1 < DEPTH and i % 2 == 0) else None
        xt, proj = _moe_ln(xt, route, m_w1, m_b1, m_w2, m_b2, ln2_g[i], ln2_b[i], i, next_gla)
        x = xt.reshape(Bsz, L, D)
    return x
```
